```python
import jax, jax.numpy as jnp
from jax import lax
import numpy as np

D_MODEL = 1024
BATCH = 4
SEQ = 4096
DEPTH = 1
DEC_BATCH = 32
DEC_SEQ = 4
PAST_LEN = 16384
PAGE_SIZE = 128

CHUNK = 128
D_A = D_MODEL
A_GROUPS = 8
A_GROUP_DIM = D_A // A_GROUPS
DIL_WINDOWS = (128, 512, 2048)
DIL_RATES = (1, 4, 16)
N_DIL = 3
B_HEADS = 8
HEAD_DIM = 64
D_B = B_HEADS * HEAD_DIM
ROPE_THETA = 10000.0
QB = 128
N_KEYS = 128
N_EXPERTS = N_KEYS * N_KEYS
PEER_HEADS = 8
PEER_TOPK = 16
D_QUERY = 256
D_HALF = D_QUERY // 2
TOK_BLOCK = 256
EPS = 1e-6
OFF_QKV = 2 * D_A
OFF_GATE = OFF_QKV + 3 * N_DIL * D_B
D_IN = OFF_GATE + 2 * D_MODEL

kernel_name = 'gated_gmlp_dilated_attn_peer_step'


def rmsnorm(x, g):
    xf = x.astype(jnp.float32)
    y = xf * lax.rsqrt(jnp.mean(xf * xf, axis=-1, keepdims=True) + EPS)
    return (y * g.astype(jnp.float32)).astype(x.dtype)


def layernorm(x, g, b):
    xf = x.astype(jnp.float32)
    xc = xf - jnp.mean(xf, axis=-1, keepdims=True)
    var = jnp.mean(xc * xc, axis=-1, keepdims=True)
    return (xc * lax.rsqrt(var + EPS) * g.astype(jnp.float32) + b.astype(jnp.float32)).astype(x.dtype)


def rope(x, pos):
    half = HEAD_DIM // 2
    inv = 1.0 / (ROPE_THETA ** (jnp.arange(half, dtype=jnp.float32) * (2.0 / HEAD_DIM)))
    ang = pos.astype(jnp.float32)[:, None] * inv[None, :]
    cos = jnp.cos(ang)[None, :, None, None, :]
    sin = jnp.sin(ang)[None, :, None, None, :]
    xf = x.astype(jnp.float32)
    x1, x2 = xf[..., :half], xf[..., half:]
    return jnp.concatenate([x1 * cos - x2 * sin, x2 * cos + x1 * sin], axis=-1).astype(x.dtype)


def project(h, pos, w_in):
    b_, s_, _ = h.shape
    z = h @ w_in
    u_a = z[..., :D_A]
    v_a = z[..., D_A:OFF_QKV]
    qkv = z[..., OFF_QKV:OFF_GATE].reshape(b_, s_, 3, N_DIL, B_HEADS, HEAD_DIM)
    g_a = z[..., OFF_GATE:OFF_GATE + D_MODEL]
    g_b = z[..., OFF_GATE + D_MODEL:]
    q = rope(qkv[:, :, 0], pos)
    k = rope(qkv[:, :, 1], pos)
    v = qkv[:, :, 2]
    return u_a, v_a, q, k, v, g_a, g_b


def spatial_gating(u, v, lnv_g, lnv_b, w_s, b_s, chunk):
    b_, s_, _ = u.shape
    u = jax.nn.gelu(u)
    v = layernorm(jax.nn.gelu(v), lnv_g, lnv_b)
    vc = v.reshape(b_, s_ // chunk, chunk, A_GROUPS, A_GROUP_DIM)
    w = w_s[:, :chunk, :chunk] * jnp.tril(jnp.ones((chunk, chunk), w_s.dtype))
    s = jnp.einsum('gij,bcjgd->bcigd', w, vc) + jnp.transpose(b_s[:, :chunk])[None, None, :, :, None]
    return u * s.reshape(b_, s_, D_A), v


def dilated_group(q, k_all, v_all, q_idx, rate, n_back):
    m = jnp.arange(n_back + 1, dtype=jnp.int32) * rate
    idx = q_idx[:, None] - m[None, :]
    valid = idx >= 0
    idx = jnp.maximum(idx, 0)
    kg = k_all[:, idx]
    vg = v_all[:, idx]
    s = jnp.einsum('bqhd,bqmhd->bhqm', q, kg).astype(jnp.float32) * (HEAD_DIM ** -0.5)
    s = jnp.where(valid[None, None], s, jnp.finfo(jnp.float32).min)
    mx = jnp.max(s, axis=-1, keepdims=True)
    p = jnp.exp(s - mx)
    den = jnp.sum(p, axis=-1)
    o = jnp.einsum('bhqm,bqmhd->bqhd', (p / den[..., None]).astype(v_all.dtype), vg)
    return o, mx[..., 0] + jnp.log(den)


def dilated_mix(q, ks, vs, q_idxs):
    outs, lses = [], []
    for g in range(N_DIL):
        o, l = dilated_group(q[:, :, g], ks[g], vs[g], q_idxs[g], DIL_RATES[g], DIL_WINDOWS[g] // DIL_RATES[g])
        outs.append(o)
        lses.append(l)
    alpha = jax.nn.softmax(jnp.stack(lses, axis=0), axis=0)
    alpha = jnp.transpose(alpha, (0, 1, 3, 2))[..., None].astype(q.dtype)
    return jnp.sum(alpha * jnp.stack(outs, axis=0), axis=0)


def dilated_attention_prompt(q, k, v):
    b_, t_ = q.shape[:2]
    ks = [k[:, :, g] for g in range(N_DIL)]
    vs = [v[:, :, g] for g in range(N_DIL)]

    def block(n):
        qb = lax.dynamic_slice_in_dim(q, n * QB, QB, axis=1)
        q_idx = n * QB + jnp.arange(QB, dtype=jnp.int32)
        return dilated_mix(qb, ks, vs, [q_idx] * N_DIL)

    o = lax.map(block, jnp.arange(t_ // QB, dtype=jnp.int32))
    return jnp.moveaxis(o, 0, 1).reshape(b_, t_, D_B), ks, vs


def dilated_attention_sample(q, k, v, cache_ks, cache_vs):
    b_, s_ = q.shape[:2]
    ks = [jnp.concatenate([cache_ks[g], k[:, :, g]], axis=1) for g in range(N_DIL)]
    vs = [jnp.concatenate([cache_vs[g], v[:, :, g]], axis=1) for g in range(N_DIL)]
    q_idxs = [cache_ks[g].shape[1] + jnp.arange(s_, dtype=jnp.int32) for g in range(N_DIL)]
    return dilated_mix(q, ks, vs, q_idxs).reshape(b_, s_, D_B), ks, vs


def peer(h, w_q, sub_keys, u_tab, v_tab):
    b_, s_, d_ = h.shape
    x = h.reshape(b_ * s_, d_)
    t_ = x.shape[0]
    kk = PEER_TOPK * PEER_TOPK
    q = (x @ w_q).reshape(t_, PEER_HEADS, 2, D_HALF)
    s = jnp.einsum('thpd,pkd->thpk', q, sub_keys).astype(jnp.float32)
    sv, si = lax.top_k(s, PEER_TOPK)
    cand = (sv[:, :, 0, :, None] + sv[:, :, 1, None, :]).reshape(t_, PEER_HEADS, kk)
    cand_idx = (si[:, :, 0, :, None] * N_KEYS + si[:, :, 1, None, :]).reshape(t_, PEER_HEADS, kk)
    fv, fi = lax.top_k(cand, PEER_TOPK)
    experts = jnp.take_along_axis(cand_idx, fi, axis=-1)
    gates = jax.nn.softmax(fv, axis=-1).astype(x.dtype)
    n_blk = -(-t_ // TOK_BLOCK)
    pad = n_blk * TOK_BLOCK - t_
    xp = jnp.pad(x, ((0, pad), (0, 0))).reshape(n_blk, TOK_BLOCK, d_)
    ep = jnp.pad(experts, ((0, pad), (0, 0), (0, 0))).reshape(n_blk, TOK_BLOCK, PEER_HEADS, PEER_TOPK)
    gp = jnp.pad(gates, ((0, pad), (0, 0), (0, 0))).reshape(n_blk, TOK_BLOCK, PEER_HEADS, PEER_TOPK)

    def block(args):
        xb, eb, gb = args
        act = jax.nn.gelu(jnp.einsum('thkd,td->thk', u_tab[eb], xb))
        return jnp.einsum('thk,thkd->td', gb * act, v_tab[eb])

    y = lax.map(block, (xp, ep, gp)).reshape(n_blk * TOK_BLOCK, d_)[:t_]
    return y.reshape(b_, s_, d_)


def layer(x, pos, cache_ks, cache_vs, norm1_g, w_in, lnv_g, lnv_b, w_s, b_s, w_a, w_b, w_o,
          norm2_g, peer_wq, peer_subkeys, peer_u, peer_v):
    b_, s_, _ = x.shape
    h = rmsnorm(x, norm1_g)
    u_a, v_a, q, k, v, g_a, g_b = project(h, pos, w_in)
    if cache_ks is None:
        chunk = CHUNK
        y_b, k_rows, v_rows = dilated_attention_prompt(q, k, v)
    else:
        chunk = s_
        y_b, k_rows, v_rows = dilated_attention_sample(q, k, v, cache_ks, cache_vs)
    y_a, v_ln = spatial_gating(u_a, v_a, lnv_g, lnv_b, w_s, b_s, chunk)
    merged = jax.nn.sigmoid(g_a) * (y_a @ w_a) + jax.nn.sigmoid(g_b) * (y_b @ w_b)
    x = x + merged @ w_o
    x = x + peer(rmsnorm(x, norm2_g), peer_wq, peer_subkeys, peer_u, peer_v)
    new_k = [k_rows[g][:, k_rows[g].shape[1] - min(DIL_WINDOWS[g], k_rows[g].shape[1]):] for g in range(N_DIL)]
    new_v = [v_rows[g][:, v_rows[g].shape[1] - min(DIL_WINDOWS[g], v_rows[g].shape[1]):] for g in range(N_DIL)]
    tail = s_ - chunk * ((s_ - 1) // chunk)
    return x, new_k, new_v, v_ln[:, s_ - tail:]


def setup_inputs(seed: int = 0) -> dict:
    key = jax.random.key(seed)
    ks = jax.random.split(key, 24)
    f32 = jnp.float32

    def nrm(k, shape, scale):
        return jax.random.normal(k, shape, f32) * scale

    def cache(k, w):
        return nrm(k, (DEPTH, DEC_BATCH, min(w, PAST_LEN), B_HEADS, HEAD_DIM), 1.0)

    return {
        'x_prompt': nrm(ks[0], (BATCH, SEQ, D_MODEL), 1.0),
        'x_sample': nrm(ks[1], (DEC_BATCH, DEC_SEQ, D_MODEL), 1.0),
        'cache_k_w128': cache(ks[2], DIL_WINDOWS[0]),
        'cache_v_w128': cache(ks[3], DIL_WINDOWS[0]),
        'cache_k_w512': cache(ks[4], DIL_WINDOWS[1]),
        'cache_v_w512': cache(ks[5], DIL_WINDOWS[1]),
        'cache_k_w2048': cache(ks[6], DIL_WINDOWS[2]),
        'cache_v_w2048': cache(ks[7], DIL_WINDOWS[2]),
        'norm1_g': 1.0 + nrm(ks[8], (DEPTH, D_MODEL), 0.05),
        'w_in': nrm(ks[9], (DEPTH, D_MODEL, D_IN), D_MODEL ** -0.5),
        'lnv_g': 1.0 + nrm(ks[10], (DEPTH, D_A), 0.05),
        'lnv_b': nrm(ks[11], (DEPTH, D_A), 0.02),
        'w_s': nrm(ks[12], (DEPTH, A_GROUPS, CHUNK, CHUNK), CHUNK ** -0.5),
        'b_s': 1.0 + nrm(ks[13], (DEPTH, A_GROUPS, CHUNK), 0.1),
        'w_a': nrm(ks[14], (DEPTH, D_A, D_MODEL), D_A ** -0.5),
        'w_b': nrm(ks[15], (DEPTH, D_B, D_MODEL), D_B ** -0.5),
        'w_o': nrm(ks[16], (DEPTH, D_MODEL, D_MODEL), D_MODEL ** -0.5),
        'norm2_g': 1.0 + nrm(ks[17], (DEPTH, D_MODEL), 0.05),
        'peer_wq': nrm(ks[18], (DEPTH, D_MODEL, PEER_HEADS * D_QUERY), D_MODEL ** -0.5),
        'peer_subkeys': nrm(ks[19], (DEPTH, 2, N_KEYS, D_HALF), D_HALF ** -0.5),
        'peer_u': nrm(ks[20], (DEPTH, N_EXPERTS, D_MODEL), D_MODEL ** -0.5),
        'peer_v': nrm(ks[21], (DEPTH, N_EXPERTS, D_MODEL), PEER_HEADS ** -0.5),
        'final_norm_g': 1.0 + nrm(ks[22], (D_MODEL,), 0.05),
    }


def _stack(rows):
    return jnp.stack(rows, axis=0)


def reference(x_prompt, x_sample, cache_k_w128, cache_v_w128, cache_k_w512, cache_v_w512,
              cache_k_w2048, cache_v_w2048, norm1_g, w_in, lnv_g, lnv_b, w_s, b_s, w_a, w_b, w_o,
              norm2_g, peer_wq, peer_subkeys, peer_u, peer_v, final_norm_g):
    cache_k = (cache_k_w128, cache_k_w512, cache_k_w2048)
    cache_v = (cache_v_w128, cache_v_w512, cache_v_w2048)
    pos_p = jnp.arange(x_prompt.shape[1], dtype=jnp.int32)
    pos_s = PAST_LEN + jnp.arange(x_sample.shape[1], dtype=jnp.int32)
    hp, hs = x_prompt, x_sample
    p_k = [[] for _ in range(N_DIL)]
    p_v = [[] for _ in range(N_DIL)]
    s_k = [[] for _ in range(N_DIL)]
    s_v = [[] for _ in range(N_DIL)]
    p_a, s_a = [], []
    for l in range(DEPTH):
        wts = (norm1_g[l], w_in[l], lnv_g[l], lnv_b[l], w_s[l], b_s[l], w_a[l], w_b[l], w_o[l],
               norm2_g[l], peer_wq[l], peer_subkeys[l], peer_u[l], peer_v[l])
        hp, nk, nv, na = layer(hp, pos_p, None, None, *wts)
        hs, mk, mv, ma = layer(hs, pos_s, [c[l] for c in cache_k], [c[l] for c in cache_v], *wts)
        for g in range(N_DIL):
            p_k[g].append(nk[g])
            p_v[g].append(nv[g])
            s_k[g].append(mk[g])
            s_v[g].append(mv[g])
        p_a.append(na)
        s_a.append(ma)
    y_prompt = rmsnorm(hp, final_norm_g)
    y_sample = rmsnorm(hs, final_norm_g)
    return (y_prompt, y_sample,
            _stack(p_k[0]), _stack(p_v[0]), _stack(p_k[1]), _stack(p_v[1]), _stack(p_k[2]), _stack(p_v[2]),
            _stack(p_a),
            _stack(s_k[0]), _stack(s_v[0]), _stack(s_k[1]), _stack(s_v[1]), _stack(s_k[2]), _stack(s_v[2]),
            _stack(s_a))
```

```python
import functools

import jax
import jax.numpy as jnp
from jax import lax
from jax.experimental import pallas as pl
from jax.experimental.pallas import tpu as pltpu

F32 = jnp.float32
BF16 = jnp.bfloat16

D_MODEL = 1024
PAST_LEN = 16384
CHUNK = 128
D_A = D_MODEL
A_GROUPS = 8
A_GROUP_DIM = D_A // A_GROUPS
DIL_WINDOWS = (128, 512, 2048)
DIL_RATES = (1, 4, 16)
N_DIL = 3
N_BACK = 128
B_HEADS = 8
HEAD_DIM = 64
D_B = B_HEADS * HEAD_DIM
ROPE_THETA = 10000.0
N_KEYS = 128
N_EXPERTS = N_KEYS * N_KEYS
PEER_HEADS = 8
PEER_TOPK = 16
D_QUERY = 256
D_HALF = D_QUERY // 2
EPS = 1e-6
OFF_QKV = 2 * D_A
OFF_GATE = OFF_QKV + 3 * N_DIL * D_B
D_IN = OFF_GATE + 2 * D_MODEL

COL_BLK = 512
N_COL_BLK = D_IN // COL_BLK
Q_BLK = OFF_QKV // COL_BLK
K_BLK = Q_BLK + N_DIL
V_BLK = K_BLK + N_DIL
GA_BLK = OFF_GATE // COL_BLK
GB_BLK = GA_BLK + D_MODEL // COL_BLK

LANES = 128
QB = 128
NEG = float(jnp.finfo(jnp.float32).min)
VMEM_LIMIT = 56 * 1024 * 1024


def _params(*sem):
    return pltpu.CompilerParams(dimension_semantics=sem, vmem_limit_bytes=VMEM_LIMIT)


def _rms(x, g):
    return x * lax.rsqrt(jnp.mean(x * x, axis=-1, keepdims=True) + EPS) * g


def _project_kernel(x_ref, g_ref, w_ref, cos_ref, sin_ref, z_ref, h_ref):
    j = pl.program_id(1)

    @pl.when(j == 0)
    def _():
        h_ref[...] = _rms(x_ref[...], g_ref[...]).astype(BF16)

    acc = jnp.dot(h_ref[...], w_ref[...], preferred_element_type=F32)
    is_rope = jnp.logical_and(j >= Q_BLK, j < V_BLK)

    @pl.when(is_rope)
    def _():
        lane = lax.broadcasted_iota(jnp.int32, acc.shape, 1)
        first_half = (lane % HEAD_DIM) < (HEAD_DIM // 2)
        partner = jnp.where(first_half,
                            pltpu.roll(acc, COL_BLK - HEAD_DIM // 2, 1),
                            pltpu.roll(acc, HEAD_DIM // 2, 1))
        z_ref[...] = acc * cos_ref[...] + partner * sin_ref[...]

    @pl.when(jnp.logical_not(is_rope))
    def _():
        z_ref[...] = acc


def _project(x, g, w_bf, cos, sin, tm):
    t = x.shape[0]
    n_pos_blk = cos.shape[0] // tm
    return pl.pallas_call(
        _project_kernel,
        grid=(t // tm, N_COL_BLK),
        in_specs=[
            pl.BlockSpec((tm, D_MODEL), lambda i, j: (i, 0)),
            pl.BlockSpec((1, D_MODEL), lambda i, j: (0, 0)),
            pl.BlockSpec((D_MODEL, COL_BLK), lambda i, j: (0, j)),
            pl.BlockSpec((tm, COL_BLK), lambda i, j: (i % n_pos_blk, 0)),
            pl.BlockSpec((tm, COL_BLK), lambda i, j: (i % n_pos_blk, 0)),
        ],
        out_specs=pl.BlockSpec((tm, COL_BLK), lambda i, j: (i, j)),
        out_shape=jax.ShapeDtypeStruct((t, D_IN), F32),
        scratch_shapes=[pltpu.VMEM((tm, D_MODEL), BF16)],
        compiler_params=_params("parallel", "arbitrary"),
        name="project",
    )(x, g, w_bf, cos, sin)


def _rope_tables(pos):
    half = HEAD_DIM // 2
    inv = 1.0 / (ROPE_THETA ** (jnp.arange(half, dtype=F32) * (2.0 / HEAD_DIM)))
    ang = pos.astype(F32)[:, None] * inv[None, :]
    cos, sin = jnp.cos(ang), jnp.sin(ang)
    cos_h = jnp.concatenate([cos, cos], axis=1)
    sin_h = jnp.concatenate([-sin, sin], axis=1)
    reps = COL_BLK // HEAD_DIM
    return jnp.tile(cos_h, (1, reps)), jnp.tile(sin_h, (1, reps))


def _softmax_rows(s):
    mx = jnp.max(s, axis=-1, keepdims=True)
    p = jnp.exp(s - mx)
    den = jnp.sum(p, axis=-1, keepdims=True)
    return p / den, mx + jnp.log(den)


def _attn_prompt_kernel(q_ref, kp_ref, kc_ref, vp_ref, vc_ref, o_ref, lse_ref):
    n = pl.program_id(2)
    row = lax.broadcasted_iota(jnp.int32, (QB, 2 * QB), 0)
    col = lax.broadcasted_iota(jnp.int32, (QB, 2 * QB), 1)
    back = QB + row - col
    first_col = jnp.where(n > 0, 0, QB)
    valid = jnp.logical_and(jnp.logical_and(back >= 0, back <= N_BACK), col >= first_col)
    lane = lax.broadcasted_iota(jnp.int32, (QB, LANES), 1)
    low_head = lane < HEAD_DIM
    scale = HEAD_DIM ** -0.5
    for pair in range(D_B // LANES):
        sl = slice(pair * LANES, (pair + 1) * LANES)
        q2 = q_ref[:, sl] * scale
        k2 = jnp.concatenate([kp_ref[:, sl], kc_ref[:, sl]], axis=0).astype(BF16)
        v2 = jnp.concatenate([vp_ref[:, sl], vc_ref[:, sl]], axis=0).astype(BF16)
        o_pair, lse_pair = [], []
        for head_mask in (low_head, jnp.logical_not(low_head)):
            qm = jnp.where(head_mask, q2, 0.0).astype(BF16)
            s = lax.dot_general(qm, k2, (((1,), (1,)), ((), ())), preferred_element_type=F32)
            p, lse = _softmax_rows(jnp.where(valid, s, NEG))
            o_pair.append(jnp.dot(p.astype(BF16), v2, preferred_element_type=F32))
            lse_pair.append(lse)
        o_ref[:, sl] = jnp.where(low_head, o_pair[0], o_pair[1])
        lse_ref[:, sl] = jnp.where(low_head, lse_pair[0], lse_pair[1])


def _attn_prompt(z, batch, seq, g):
    r = DIL_RATES[g]
    t = batch * seq
    nb = seq // r // QB
    zr = z.reshape(t // r, r * D_IN)
    blk = (QB, COL_BLK)

    def spec(col_blk, prev):
        def index(b, rho, n):
            return (b * nb + (jnp.maximum(n - 1, 0) if prev else n), rho * N_COL_BLK + col_blk + g)
        return pl.BlockSpec(blk, index)

    out_spec = pl.BlockSpec(blk, lambda b, rho, n: (b * nb + n, rho))
    o, lse = pl.pallas_call(
        _attn_prompt_kernel,
        grid=(batch, r, nb),
        in_specs=[spec(Q_BLK, False), spec(K_BLK, True), spec(K_BLK, False),
                  spec(V_BLK, True), spec(V_BLK, False)],
        out_specs=[out_spec, out_spec],
        out_shape=[jax.ShapeDtypeStruct((t // r, r * D_B), F32)] * 2,
        compiler_params=_params("parallel", "parallel", "arbitrary"),
        name=f"attn_prompt_r{r}",
    )(zr, zr, zr, zr, zr)
    return o.reshape(t, D_B), lse.reshape(t, D_B)


def _attn_sample_kernel(dec_seq, *refs):
    q_refs, kn_refs, vn_refs = refs[0:3], refs[3:6], refs[6:9]
    kc_refs, vc_refs = refs[9:12], refs[12:15]
    o_refs, lse_refs = refs[15:18], refs[18:21]
    b = pl.program_id(0)
    rows = dec_seq * B_HEADS
    t_new = kn_refs[0].shape[0]
    row_h = lax.broadcasted_iota(jnp.int32, (B_HEADS, D_B), 0)
    lane_h = lax.broadcasted_iota(jnp.int32, (B_HEADS, D_B), 1) // HEAD_DIM
    head_mask = (row_h == lane_h).astype(F32)
    head_mask_rows = jnp.concatenate([head_mask] * dec_seq, axis=0)
    scale = HEAD_DIM ** -0.5
    for g in range(N_DIL):
        r, w = DIL_RATES[g], DIL_WINDOWS[g]
        q = q_refs[g][0] * scale
        q_bd = jnp.concatenate(
            [jnp.broadcast_to(q[s:s + 1, :], (B_HEADS, D_B)) * head_mask for s in range(dec_seq)],
            axis=0).astype(BF16)
        s_c = lax.dot_general(q_bd, kc_refs[g][0].astype(BF16), (((1,), (1,)), ((), ())),
                              preferred_element_type=F32)
        s_n = lax.dot_general(q_bd, kn_refs[g][...].astype(BF16), (((1,), (1,)), ((), ())),
                              preferred_element_type=F32)
        q_pos_c = lax.broadcasted_iota(jnp.int32, (rows, w), 0) // B_HEADS
        back_c = w + q_pos_c - lax.broadcasted_iota(jnp.int32, (rows, w), 1)
        valid_c = jnp.logical_and(jnp.bitwise_and(back_c, r - 1) == 0, back_c <= N_BACK * r)
        q_pos_n = lax.broadcasted_iota(jnp.int32, (rows, t_new), 0) // B_HEADS
        col_n = lax.broadcasted_iota(jnp.int32, (rows, t_new), 1)
        back_n = q_pos_n - (col_n - b * dec_seq)
        valid_n = jnp.logical_and(jnp.logical_and(back_n >= 0, col_n >= b * dec_seq),
                                  jnp.bitwise_and(back_n, r - 1) == 0)
        s_all = jnp.concatenate([jnp.where(valid_c, s_c, NEG), jnp.where(valid_n, s_n, NEG)], axis=1)
        p, lse = _softmax_rows(s_all)
        p = p.astype(BF16)
        o = (jnp.dot(p[:, :w], vc_refs[g][0].astype(BF16), preferred_element_type=F32)
             + jnp.dot(p[:, w:], vn_refs[g][...].astype(BF16), preferred_element_type=F32))
        o_refs[g][0] = jnp.sum((o * head_mask_rows).reshape(dec_seq, B_HEADS, D_B), axis=1)
        lse_refs[g][0] = jnp.sum((lse * head_mask_rows).reshape(dec_seq, B_HEADS, D_B), axis=1)


def _attn_sample(z, dec_batch, dec_seq, cache_k, cache_v):
    t = dec_batch * dec_seq
    z3 = z.reshape(dec_batch, dec_seq, D_IN)

    def row_spec(col_blk):
        return pl.BlockSpec((1, dec_seq, COL_BLK), lambda b: (b, 0, col_blk))

    def all_spec(col_blk):
        return pl.BlockSpec((t, COL_BLK), lambda b: (0, col_blk))

    def cache_spec(w):
        return pl.BlockSpec((1, w, D_B), lambda b: (b, 0, 0))

    out_spec = pl.BlockSpec((1, dec_seq, D_B), lambda b: (b, 0, 0))
    outs = pl.pallas_call(
        functools.partial(_attn_sample_kernel, dec_seq),
        grid=(dec_batch,),
        in_specs=([row_spec(Q_BLK + g) for g in range(N_DIL)]
                  + [all_spec(K_BLK + g) for g in range(N_DIL)]
                  + [all_spec(V_BLK + g) for g in range(N_DIL)]
                  + [cache_spec(w) for w in DIL_WINDOWS] * 2),
        out_specs=[out_spec] * (2 * N_DIL),
        out_shape=[jax.ShapeDtypeStruct((dec_batch, dec_seq, D_B), F32)] * (2 * N_DIL),
        compiler_params=_params("parallel"),
        name="attn_sample",
    )(*([z3] * N_DIL + [z] * (2 * N_DIL) + list(cache_k) + list(cache_v)))
    return [a.reshape(t, D_B) for a in outs]


def _mix_kernel(u_ref, v_ref, ga0_ref, ga1_ref, gb0_ref, gb1_ref,
                o0_ref, o1_ref, o2_ref, l0_ref, l1_ref, l2_ref, x_ref,
                lng_ref, lnb_ref, ws_ref, bst_ref, wa_ref, wb_ref, wo_ref,
                x1_ref, vln_ref):
    tm = u_ref.shape[0]
    u = jax.nn.gelu(u_ref[...])
    v = jax.nn.gelu(v_ref[...])
    vc = v - jnp.mean(v, axis=-1, keepdims=True)
    var = jnp.mean(vc * vc, axis=-1, keepdims=True)
    vln = vc * lax.rsqrt(var + EPS) * lng_ref[...] + lnb_ref[...]
    vln_ref[...] = vln

    row = lax.broadcasted_iota(jnp.int32, (CHUNK, CHUNK), 0)
    col = lax.broadcasted_iota(jnp.int32, (CHUNK, CHUNK), 1)
    causal = row >= col
    vln_bf = vln.astype(BF16)
    chunks = []
    for c in range(tm // CHUNK):
        groups = []
        for g in range(A_GROUPS):
            w = jnp.where(causal, ws_ref[g], 0.0).astype(BF16)
            vg = vln_bf[c * CHUNK:(c + 1) * CHUNK, g * A_GROUP_DIM:(g + 1) * A_GROUP_DIM]
            groups.append(jnp.dot(w, vg, preferred_element_type=F32) + bst_ref[:, g:g + 1])
        chunks.append(jnp.concatenate(groups, axis=1))
    y_a = u * jnp.concatenate(chunks, axis=0)

    lses = (l0_ref[...], l1_ref[...], l2_ref[...])
    outs = (o0_ref[...], o1_ref[...], o2_ref[...])
    mx = jnp.maximum(jnp.maximum(lses[0], lses[1]), lses[2])
    es = [jnp.exp(l - mx) for l in lses]
    den = es[0] + es[1] + es[2]
    y_b = (es[0] / den) * outs[0] + (es[1] / den) * outs[1] + (es[2] / den) * outs[2]

    a = jnp.dot(y_a.astype(BF16), wa_ref[...], preferred_element_type=F32)
    bproj = jnp.dot(y_b.astype(BF16), wb_ref[...], preferred_element_type=F32)
    g_a = jnp.concatenate([ga0_ref[...], ga1_ref[...]], axis=1)
    g_b = jnp.concatenate([gb0_ref[...], gb1_ref[...]], axis=1)
    merged = jax.nn.sigmoid(g_a) * a + jax.nn.sigmoid(g_b) * bproj
    x1_ref[...] = x_ref[...] + jnp.dot(merged.astype(BF16), wo_ref[...], preferred_element_type=F32)


def _mix(z, attn, x, lnv_g, lnv_b, w_s, b_s_t, w_a, w_b, w_o, tm):
    t = x.shape[0]
    wide = lambda blk: pl.BlockSpec((tm, D_MODEL), lambda i: (i, blk))
    narrow = lambda blk: pl.BlockSpec((tm, COL_BLK), lambda i: (i, blk))
    full = lambda a: pl.BlockSpec(a.shape, lambda i: (0,) * a.ndim)
    weights = (lnv_g, lnv_b, w_s, b_s_t, w_a, w_b, w_o)
    return pl.pallas_call(
        _mix_kernel,
        grid=(t // tm,),
        in_specs=([wide(0), wide(1), narrow(GA_BLK), narrow(GA_BLK + 1), narrow(GB_BLK), narrow(GB_BLK + 1)]
                  + [narrow(0)] * (2 * N_DIL) + [wide(0)] + [full(a) for a in weights]),
        out_specs=[wide(0), wide(0)],
        out_shape=[jax.ShapeDtypeStruct((t, D_MODEL), F32)] * 2,
        compiler_params=_params("parallel"),
        name="mix",
    )(z, z, z, z, z, z, *attn, x, *weights)


def _top_rows(s, k):
    n, cols = s.shape
    row = lax.broadcasted_iota(jnp.int32, (n, cols), 0)
    rank_row = lax.broadcasted_iota(jnp.int32, (k, cols), 0)

    def body(r, carry):
        work, rank, vals = carry
        m = jnp.max(work, axis=0, keepdims=True)
        first = jnp.min(jnp.where(work == m, row, n), axis=0, keepdims=True)
        sel = row == first
        return (jnp.where(sel, -jnp.inf, work), jnp.where(sel, r, rank), jnp.where(rank_row == r, m, vals))

    init = (s, jnp.full((n, cols), k, jnp.int32), jnp.zeros((k, cols), F32))
    _, rank, vals = lax.fori_loop(0, k, body, init)
    return rank, vals


def _peer_select_kernel(x_ref, g_ref, wq_ref, sk_ref, ht_ref, a_ref, nb_ref, b_ref, r1_ref, qp_ref):
    k = PEER_TOPK
    h2 = _rms(x_ref[...], g_ref[...])
    ht_ref[...] = h2.T.astype(BF16)
    qp_ref[...] = jnp.dot(h2.astype(BF16), wq_ref[...], preferred_element_type=F32).astype(BF16)

    def head(h, carry):
        s, rank, vals = [], [], []
        for p in range(2):
            off = pl.multiple_of((h * 2 + p) * D_HALF, D_HALF)
            sp = lax.dot_general(sk_ref[p], qp_ref[:, pl.ds(off, D_HALF)], (((1,), (1,)), ((), ())),
                                 preferred_element_type=F32)
            rp, vp = _top_rows(sp, k)
            s.append(sp)
            rank.append(rp)
            vals.append(vp)
        cand = jnp.concatenate([vals[0][a:a + 1, :] + vals[1] for a in range(k)], axis=0)
        pair_rank, pair_vals = _top_rows(cand, k)
        taken = jnp.where(pair_rank < k, 1.0, 0.0)
        z = jnp.sum(taken * jnp.exp(cand - pair_vals[0:1, :]), axis=0, keepdims=True)
        n_b = [jnp.sum(taken[a * k:(a + 1) * k, :], axis=0, keepdims=True) for a in range(k)]
        nb_key = jnp.zeros(s[0].shape, F32)
        for a in range(k):
            nb_key = jnp.where(rank[0] == a, n_b[a], nb_key)
        e0 = jnp.where(rank[0] < k, jnp.exp(s[0] - vals[0][0:1, :]), 0.0)
        e1 = jnp.where(rank[1] < k, jnp.exp(s[1] - vals[1][0:1, :]), 0.0)
        a_ref[h] = e0 / z
        nb_ref[h] = nb_key
        b_ref[h] = e1
        r1_ref[h] = rank[1].astype(F32)
        return carry

    lax.fori_loop(0, PEER_HEADS, head, 0)


def _peer_select(x1, g2, wq_bf, sk_bf, tb):
    t = x1.shape[0]
    sel_spec = pl.BlockSpec((PEER_HEADS, N_KEYS, tb), lambda i: (0, 0, i))
    sel_shape = jax.ShapeDtypeStruct((PEER_HEADS, N_KEYS, t), F32)
    return pl.pallas_call(
        _peer_select_kernel,
        grid=(t // tb,),
        in_specs=[
            pl.BlockSpec((tb, D_MODEL), lambda i: (i, 0)),
            pl.BlockSpec((1, D_MODEL), lambda i: (0, 0)),
            pl.BlockSpec(wq_bf.shape, lambda i: (0, 0)),
            pl.BlockSpec(sk_bf.shape, lambda i: (0, 0, 0)),
        ],
        out_specs=[pl.BlockSpec((D_MODEL, tb), lambda i: (0, i))] + [sel_spec] * 4,
        out_shape=[jax.ShapeDtypeStruct((D_MODEL, t), BF16)] + [sel_shape] * 4,
        scratch_shapes=[pltpu.VMEM((tb, PEER_HEADS * D_QUERY), BF16)],
        compiler_params=_params("parallel"),
        name="peer_select",
    )(x1, g2, wq_bf, sk_bf)


def _peer_dense_kernel(ht_ref, u_ref, vt_ref, a_ref, nb_ref, b_ref, r1_ref, x1_ref, gf_ref,
                       y_ref, acc_ref, gact_ref):
    c = pl.program_id(1)

    @pl.when(c == 0)
    def _():
        acc_ref[...] = jnp.zeros_like(acc_ref)

    act = jax.nn.gelu(jnp.dot(u_ref[...], ht_ref[...], preferred_element_type=F32))
    for il in range(u_ref.shape[0] // N_KEYS):
        rows = slice(il * N_KEYS, (il + 1) * N_KEYS)
        gate = None
        for h in range(PEER_HEADS):
            term = jnp.where(r1_ref[h] < nb_ref[h, il:il + 1, :], a_ref[h, il:il + 1, :] * b_ref[h], 0.0)
            gate = term if gate is None else gate + term
        gact_ref[rows, :] = (gate * act[rows, :]).astype(BF16)
    acc_ref[...] += jnp.dot(vt_ref[...], gact_ref[...], preferred_element_type=F32)

    @pl.when(c == pl.num_programs(1) - 1)
    def _():
        y_ref[...] = _rms(x1_ref[...] + acc_ref[...].T, gf_ref[...])


def _peer_dense(ht, u_bf, vt_bf, sel, x1, gf, tb, ec):
    t = x1.shape[0]
    a, nb, b, r1 = sel
    key_rows = ec // N_KEYS
    return pl.pallas_call(
        _peer_dense_kernel,
        grid=(t // tb, N_EXPERTS // ec),
        in_specs=[
            pl.BlockSpec((D_MODEL, tb), lambda i, c: (0, i)),
            pl.BlockSpec((ec, D_MODEL), lambda i, c: (c, 0)),
            pl.BlockSpec((D_MODEL, ec), lambda i, c: (0, c)),
            pl.BlockSpec((PEER_HEADS, key_rows, tb), lambda i, c: (0, c, i)),
            pl.BlockSpec((PEER_HEADS, key_rows, tb), lambda i, c: (0, c, i)),
            pl.BlockSpec((PEER_HEADS, N_KEYS, tb), lambda i, c: (0, 0, i)),
            pl.BlockSpec((PEER_HEADS, N_KEYS, tb), lambda i, c: (0, 0, i)),
            pl.BlockSpec((tb, D_MODEL), lambda i, c: (i, 0)),
            pl.BlockSpec((1, D_MODEL), lambda i, c: (0, 0)),
        ],
        out_specs=pl.BlockSpec((tb, D_MODEL), lambda i, c: (i, 0)),
        out_shape=jax.ShapeDtypeStruct((t, D_MODEL), F32),
        scratch_shapes=[pltpu.VMEM((D_MODEL, tb), F32), pltpu.VMEM((ec, tb), BF16)],
        compiler_params=_params("parallel", "arbitrary"),
        name="peer_dense",
    )(ht, u_bf, vt_bf, a, nb, b, r1, x1, gf)


def _tile(t, pref):
    return pref if t % pref == 0 else t


def _layer_tail(z, attn, x, wts, chunk_w, chunk_b_t):
    t = x.shape[0]
    x1, vln = _mix(z, attn, x, wts["lnv_g"], wts["lnv_b"], chunk_w, chunk_b_t,
                   wts["w_a"], wts["w_b"], wts["w_o"], _tile(t, 256))
    tb = _tile(t, 256)
    ht, *sel = _peer_select(x1, wts["norm2_g"], wts["peer_wq"], wts["peer_subkeys"], tb)
    tb = _tile(t, 512)
    y = _peer_dense(ht, wts["peer_u"], wts["peer_vt"], sel, x1, wts["final_norm_g"], tb, 1024)
    return y, vln


def kernel(x_prompt, x_sample, cache_k_w128, cache_v_w128, cache_k_w512, cache_v_w512,
           cache_k_w2048, cache_v_w2048, norm1_g, w_in, lnv_g, lnv_b, w_s, b_s, w_a, w_b, w_o,
           norm2_g, peer_wq, peer_subkeys, peer_u, peer_v, final_norm_g):
    depth = w_in.shape[0]
    assert depth == 1, "one layer: the final rmsnorm is fused into the layer's last kernel"
    batch, seq, _ = x_prompt.shape
    dec_batch, dec_seq, _ = x_sample.shape
    tp, ts = batch * seq, dec_batch * dec_seq
    assert seq % (QB * DIL_RATES[-1]) == 0 and dec_seq <= min(DIL_RATES[1:]) and CHUNK % dec_seq == 0
    assert ts % CHUNK == 0 and all(c.shape[2] == w for c, w in zip((cache_k_w128, cache_k_w512, cache_k_w2048), DIL_WINDOWS))
    l = 0
    row = lambda a: a[l].reshape(1, -1)
    wts = {
        "lnv_g": row(lnv_g), "lnv_b": row(lnv_b),
        "w_a": w_a[l].astype(BF16), "w_b": w_b[l].astype(BF16), "w_o": w_o[l].astype(BF16),
        "norm2_g": row(norm2_g), "peer_wq": peer_wq[l].astype(BF16),
        "peer_subkeys": peer_subkeys[l].astype(BF16),
        "peer_u": peer_u[l].astype(BF16), "peer_vt": peer_v[l].T.astype(BF16),
        "final_norm_g": final_norm_g.reshape(1, -1),
    }
    g1 = row(norm1_g)
    w_in_bf = w_in[l].astype(BF16)

    xp = x_prompt.reshape(tp, D_MODEL)
    cos_p, sin_p = _rope_tables(jnp.arange(seq, dtype=jnp.int32))
    zp = _project(xp, g1, w_in_bf, cos_p, sin_p, 1024)
    attn_p = [_attn_prompt(zp, batch, seq, g) for g in range(N_DIL)]
    attn_p = [a[0] for a in attn_p] + [a[1] for a in attn_p]
    yp, vln_p = _layer_tail(zp, attn_p, xp, wts, w_s[l], jnp.transpose(b_s[l]))

    xs = x_sample.reshape(ts, D_MODEL)
    pos_s = PAST_LEN + jnp.tile(jnp.arange(dec_seq, dtype=jnp.int32), dec_batch)
    cos_s, sin_s = _rope_tables(pos_s)
    zs = _project(xs, g1, w_in_bf, cos_s, sin_s, ts)
    cache_k = [c[l].reshape(dec_batch, w, D_B) for c, w in zip((cache_k_w128, cache_k_w512, cache_k_w2048), DIL_WINDOWS)]
    cache_v = [c[l].reshape(dec_batch, w, D_B) for c, w in zip((cache_v_w128, cache_v_w512, cache_v_w2048), DIL_WINDOWS)]
    attn_s = _attn_sample(zs, dec_batch, dec_seq, cache_k, cache_v)
    eye = jnp.eye(CHUNK // dec_seq, dtype=F32)
    w_s_blk = jax.vmap(lambda w: jnp.kron(eye, w))(w_s[l][:, :dec_seq, :dec_seq])
    b_s_blk_t = jnp.transpose(jnp.tile(b_s[l][:, :dec_seq], (1, CHUNK // dec_seq)))
    ys, vln_s = _layer_tail(zs, attn_s, xs, wts, w_s_blk, b_s_blk_t)

    def heads(a, b_, rows):
        return a.reshape(1, b_, rows, B_HEADS, HEAD_DIM)

    zp3 = zp.reshape(batch, seq, D_IN)
    zs3 = zs.reshape(dec_batch, dec_seq, D_IN)
    prompt_kv, sample_kv = [], []
    for g, w in enumerate(DIL_WINDOWS):
        keep = min(w, seq)
        for blk, cache in ((K_BLK, cache_k[g]), (V_BLK, cache_v[g])):
            cols = slice((blk + g) * COL_BLK, (blk + g + 1) * COL_BLK)
            prompt_kv.append(heads(zp3[:, seq - keep:, cols], batch, keep))
            rows = jnp.concatenate([cache, zs3[:, :, cols]], axis=1)
            keep_s = min(w, rows.shape[1])
            sample_kv.append(heads(rows[:, rows.shape[1] - keep_s:], dec_batch, keep_s))
    tail = seq - CHUNK * ((seq - 1) // CHUNK)
    prompt_gmlp_v = vln_p.reshape(1, batch, seq, D_A)[:, :, seq - tail:]
    sample_gmlp_v = vln_s.reshape(1, dec_batch, dec_seq, D_A)
    return (yp.reshape(batch, seq, D_MODEL), ys.reshape(dec_batch, dec_seq, D_MODEL),
            *prompt_kv, prompt_gmlp_v, *sample_kv, sample_gmlp_v)
```

```python
import functools

import jax
import jax.numpy as jnp
from jax import lax
from jax.experimental import pallas as pl
from jax.experimental.pallas import tpu as pltpu

F32 = jnp.float32
BF16 = jnp.bfloat16

D_MODEL = 1024
PAST_LEN = 16384
CHUNK = 128
D_A = D_MODEL
A_GROUPS = 8
A_GROUP_DIM = D_A // A_GROUPS
DIL_WINDOWS = (128, 512, 2048)
DIL_RATES = (1, 4, 16)
N_DIL = 3
N_BACK = 128
B_HEADS = 8
HEAD_DIM = 64
D_B = B_HEADS * HEAD_DIM
ROPE_THETA = 10000.0
N_KEYS = 128
N_EXPERTS = N_KEYS * N_KEYS
PEER_HEADS = 8
PEER_TOPK = 16
D_QUERY = 256
D_HALF = D_QUERY // 2
EPS = 1e-6
OFF_QKV = 2 * D_A
OFF_GATE = OFF_QKV + 3 * N_DIL * D_B
D_IN = OFF_GATE + 2 * D_MODEL

COL_BLK = 512
N_COL_BLK = D_IN // COL_BLK
Q_BLK = OFF_QKV // COL_BLK
K_BLK = Q_BLK + N_DIL
V_BLK = K_BLK + N_DIL
GA_BLK = OFF_GATE // COL_BLK
GB_BLK = GA_BLK + D_MODEL // COL_BLK

LANES = 128
BF16_ROWS = 16
QB = 128
SUPER = QB * DIL_RATES[-1]
ATTN_UNROLL = 4
NEG = float(jnp.finfo(jnp.float32).min)
VMEM_LIMIT = 56 * 1024 * 1024


def _params(*sem):
    return pltpu.CompilerParams(dimension_semantics=sem, vmem_limit_bytes=VMEM_LIMIT)


def _rms(x, g):
    return x * lax.rsqrt(jnp.mean(x * x, axis=-1, keepdims=True) + EPS) * g


def _gelu_tanh(x):
    c = -2.0 * (2.0 / jnp.pi) ** 0.5 * 1.4426950408889634
    return x / (1.0 + jnp.exp2(x * (c + (c * 0.044715) * (x * x))))


def _project_kernel(x_ref, g_ref, w_ref, cos_ref, sin_ref, z_ref, h_ref):
    j = pl.program_id(1)

    @pl.when(j == 0)
    def _():
        h_ref[...] = _rms(x_ref[...], g_ref[...]).astype(BF16)

    acc = jnp.dot(h_ref[...], w_ref[...], preferred_element_type=F32)
    is_rope = jnp.logical_and(j >= Q_BLK, j < V_BLK)

    @pl.when(is_rope)
    def _():
        lane = lax.broadcasted_iota(jnp.int32, acc.shape, 1)
        first_half = (lane % HEAD_DIM) < (HEAD_DIM // 2)
        partner = jnp.where(first_half,
                            pltpu.roll(acc, COL_BLK - HEAD_DIM // 2, 1),
                            pltpu.roll(acc, HEAD_DIM // 2, 1))
        z_ref[...] = acc * cos_ref[...] + partner * sin_ref[...]

    @pl.when(jnp.logical_not(is_rope))
    def _():
        z_ref[...] = acc


def _project(x, g, w_bf, cos, sin, tm):
    t = x.shape[0]
    n_pos_blk = cos.shape[0] // tm
    return pl.pallas_call(
        _project_kernel,
        grid=(t // tm, N_COL_BLK),
        in_specs=[
            pl.BlockSpec((tm, D_MODEL), lambda i, j: (i, 0)),
            pl.BlockSpec((1, D_MODEL), lambda i, j: (0, 0)),
            pl.BlockSpec((D_MODEL, COL_BLK), lambda i, j: (0, j)),
            pl.BlockSpec((tm, COL_BLK), lambda i, j: (i % n_pos_blk, 0)),
            pl.BlockSpec((tm, COL_BLK), lambda i, j: (i % n_pos_blk, 0)),
        ],
        out_specs=pl.BlockSpec((tm, COL_BLK), lambda i, j: (i, j)),
        out_shape=jax.ShapeDtypeStruct((t, D_IN), F32),
        scratch_shapes=[pltpu.VMEM((tm, D_MODEL), BF16)],
        compiler_params=_params("parallel", "arbitrary"),
        name="project",
    )(x, g, w_bf, cos, sin)


def _rope_tables(pos):
    half = HEAD_DIM // 2
    inv = 1.0 / (ROPE_THETA ** (jnp.arange(half, dtype=F32) * (2.0 / HEAD_DIM)))
    ang = pos.astype(F32)[:, None] * inv[None, :]
    cos, sin = jnp.cos(ang), jnp.sin(ang)
    cos_h = jnp.concatenate([cos, cos], axis=1)
    sin_h = jnp.concatenate([-sin, sin], axis=1)
    reps = COL_BLK // HEAD_DIM
    return jnp.tile(cos_h, (1, reps)), jnp.tile(sin_h, (1, reps))


def _softmax_rows(s):
    mx = jnp.max(s, axis=-1, keepdims=True)
    p = jnp.exp(s - mx)
    den = jnp.sum(p, axis=-1, keepdims=True)
    return p / den, mx + jnp.log(den)


def _combine_groups(outs, lses):
    mx = functools.reduce(jnp.maximum, lses)
    es = [jnp.exp(l - mx) for l in lses]
    den = functools.reduce(jnp.add, es)
    return functools.reduce(jnp.add, [(e / den) * o for e, o in zip(es, outs)])


def _attn_head_pair(q, kp, kc, vp, vc, valid):
    lane = lax.broadcasted_iota(jnp.int32, (QB, LANES), 1)
    low_head = lane < HEAD_DIM
    q2 = q * (HEAD_DIM ** -0.5)
    k2 = jnp.concatenate([kp, kc], axis=0).astype(BF16)
    v2 = jnp.concatenate([vp, vc], axis=0).astype(BF16)
    o_pair, lse_pair = [], []
    for head_mask in (low_head, jnp.logical_not(low_head)):
        qm = jnp.where(head_mask, q2, 0.0).astype(BF16)
        s = lax.dot_general(qm, k2, (((1,), (1,)), ((), ())), preferred_element_type=F32)
        p, lse = _softmax_rows(jnp.where(valid, s, NEG))
        o_pair.append(jnp.dot(p.astype(BF16), v2, preferred_element_type=F32))
        lse_pair.append(lse)
    return jnp.where(low_head, o_pair[0], o_pair[1]), jnp.where(low_head, lse_pair[0], lse_pair[1])


def _attn_prompt_kernel(*refs):
    n_g = N_DIL
    q_refs, kc_refs, vc_refs = refs[0:n_g], refs[n_g:2 * n_g], refs[2 * n_g:3 * n_g]
    kp_refs, vp_refs = refs[3 * n_g:4 * n_g], refs[4 * n_g:5 * n_g]
    y_ref, o_scr, lse_scr = refs[5 * n_g:]
    n = pl.program_id(1)
    row = lax.broadcasted_iota(jnp.int32, (QB, 2 * QB), 0)
    col = lax.broadcasted_iota(jnp.int32, (QB, 2 * QB), 1)
    back = QB + row - col
    in_band = jnp.logical_and(back >= 0, back <= N_BACK)
    first_col = jnp.where(n > 0, 0, QB)
    in_band_first = jnp.logical_and(in_band, col >= first_col)

    for g in range(n_g):
        r = DIL_RATES[g]
        span = QB * r

        def rows_at(start, r=r):
            return pl.ds(start, QB, stride=r) if r > 1 else pl.ds(start, QB)

        def store(g, rows, result):
            o_scr[g, rows, :], lse_scr[g, rows, :] = result

        def first_item(rho, carry, g=g, rows_at=rows_at):
            rows = rows_at(rho)
            store(g, rows, _attn_head_pair(q_refs[g][rows, :], kp_refs[g][rows, :], kc_refs[g][rows, :],
                                           vp_refs[g][rows, :], vc_refs[g][rows, :], in_band_first))
            return carry

        def inner_item(i, carry, g=g, r=r, span=span, rows_at=rows_at):
            start = (1 + i // r) * span + i % r
            rows, prev = rows_at(start), rows_at(start - span)
            store(g, rows, _attn_head_pair(q_refs[g][rows, :], kc_refs[g][prev, :], kc_refs[g][rows, :],
                                           vc_refs[g][prev, :], vc_refs[g][rows, :], in_band))
            return carry

        n_inner = (SUPER // span - 1) * r
        lax.fori_loop(0, r, first_item, 0, unroll=min(r, ATTN_UNROLL))
        if n_inner:
            lax.fori_loop(0, n_inner, inner_item, 0,
                          unroll=max(u for u in range(1, ATTN_UNROLL + 2) if n_inner % u == 0))

    y_ref[...] = _combine_groups([o_scr[g] for g in range(n_g)], [lse_scr[g] for g in range(n_g)])


def _attn_prompt(z, batch, seq):
    t = batch * seq
    n_super = seq // SUPER
    lane_blks = COL_BLK // LANES

    def cur(col_blk):
        return pl.BlockSpec((SUPER, LANES), lambda b, n, hp: (b * n_super + n, col_blk * lane_blks + hp))

    def prev(col_blk, r):
        span = QB * r
        per_super, per_seq = SUPER // span, seq // span
        return pl.BlockSpec((span, LANES), lambda b, n, hp: (b * per_seq + jnp.maximum(n * per_super - 1, 0),
                                                             col_blk * lane_blks + hp))

    groups = range(N_DIL)
    return pl.pallas_call(
        _attn_prompt_kernel,
        grid=(batch, n_super, D_B // LANES),
        in_specs=([cur(Q_BLK + g) for g in groups] + [cur(K_BLK + g) for g in groups]
                  + [cur(V_BLK + g) for g in groups]
                  + [prev(K_BLK + g, DIL_RATES[g]) for g in groups]
                  + [prev(V_BLK + g, DIL_RATES[g]) for g in groups]),
        out_specs=pl.BlockSpec((SUPER, LANES), lambda b, n, hp: (b * n_super + n, hp)),
        out_shape=jax.ShapeDtypeStruct((t, D_B), F32),
        scratch_shapes=[pltpu.VMEM((N_DIL, SUPER, LANES), F32)] * 2,
        compiler_params=_params("parallel", "parallel", "arbitrary"),
        name="attn_prompt",
    )(*([z] * (5 * N_DIL)))


def _attn_sample_kernel(dec_seq, *refs):
    q_refs, kn_refs, vn_refs = refs[0:3], refs[3:6], refs[6:9]
    kc_refs, vc_refs = refs[9:12], refs[12:15]
    y_ref = refs[15]
    outs, lses = [], []
    b = pl.program_id(0)
    rows = dec_seq * B_HEADS
    t_new = kn_refs[0].shape[0]
    row_h = lax.broadcasted_iota(jnp.int32, (B_HEADS, D_B), 0)
    lane_h = lax.broadcasted_iota(jnp.int32, (B_HEADS, D_B), 1) // HEAD_DIM
    head_mask = (row_h == lane_h).astype(F32)
    head_mask_rows = jnp.concatenate([head_mask] * dec_seq, axis=0)
    scale = HEAD_DIM ** -0.5
    for g in range(N_DIL):
        r, w = DIL_RATES[g], DIL_WINDOWS[g]
        q = q_refs[g][0] * scale
        q_bd = jnp.concatenate(
            [jnp.broadcast_to(q[s:s + 1, :], (B_HEADS, D_B)) * head_mask for s in range(dec_seq)],
            axis=0).astype(BF16)
        s_c = lax.dot_general(q_bd, kc_refs[g][0].astype(BF16), (((1,), (1,)), ((), ())),
                              preferred_element_type=F32)
        s_n = lax.dot_general(q_bd, kn_refs[g][...].astype(BF16), (((1,), (1,)), ((), ())),
                              preferred_element_type=F32)
        q_pos_c = lax.broadcasted_iota(jnp.int32, (rows, w), 0) // B_HEADS
        back_c = w + q_pos_c - lax.broadcasted_iota(jnp.int32, (rows, w), 1)
        valid_c = jnp.logical_and(jnp.bitwise_and(back_c, r - 1) == 0, back_c <= N_BACK * r)
        q_pos_n = lax.broadcasted_iota(jnp.int32, (rows, t_new), 0) // B_HEADS
        col_n = lax.broadcasted_iota(jnp.int32, (rows, t_new), 1)
        back_n = q_pos_n - (col_n - b * dec_seq)
        valid_n = jnp.logical_and(jnp.logical_and(back_n >= 0, col_n >= b * dec_seq),
                                  jnp.bitwise_and(back_n, r - 1) == 0)
        s_all = jnp.concatenate([jnp.where(valid_c, s_c, NEG), jnp.where(valid_n, s_n, NEG)], axis=1)
        p, lse = _softmax_rows(s_all)
        p = p.astype(BF16)
        o = (jnp.dot(p[:, :w], vc_refs[g][0].astype(BF16), preferred_element_type=F32)
             + jnp.dot(p[:, w:], vn_refs[g][...].astype(BF16), preferred_element_type=F32))
        outs.append(o)
        lses.append(lse)
    y = _combine_groups(outs, lses)
    y_ref[0] = jnp.sum((y * head_mask_rows).reshape(dec_seq, B_HEADS, D_B), axis=1)


def _attn_sample(z, dec_batch, dec_seq, cache_k, cache_v):
    t = dec_batch * dec_seq
    z3 = z.reshape(dec_batch, dec_seq, D_IN)

    def row_spec(col_blk):
        return pl.BlockSpec((1, dec_seq, COL_BLK), lambda b: (b, 0, col_blk))

    def all_spec(col_blk):
        return pl.BlockSpec((t, COL_BLK), lambda b: (0, col_blk))

    def cache_spec(w):
        return pl.BlockSpec((1, w, D_B), lambda b: (b, 0, 0))

    y = pl.pallas_call(
        functools.partial(_attn_sample_kernel, dec_seq),
        grid=(dec_batch,),
        in_specs=([row_spec(Q_BLK + g) for g in range(N_DIL)]
                  + [all_spec(K_BLK + g) for g in range(N_DIL)]
                  + [all_spec(V_BLK + g) for g in range(N_DIL)]
                  + [cache_spec(w) for w in DIL_WINDOWS] * 2),
        out_specs=pl.BlockSpec((1, dec_seq, D_B), lambda b: (b, 0, 0)),
        out_shape=jax.ShapeDtypeStruct((dec_batch, dec_seq, D_B), F32),
        compiler_params=_params("parallel"),
        name="attn_sample",
    )(*([z3] * N_DIL + [z] * (2 * N_DIL) + list(cache_k) + list(cache_v)))
    return y.reshape(t, D_B)


def _mix_kernel(u_ref, v_ref, ga0_ref, ga1_ref, gb0_ref, gb1_ref,
                yb_ref, x_ref,
                lng_ref, lnb_ref, ws_ref, bst_ref, wa_ref, wb_ref, wo_ref,
                x1_ref, vln_ref):
    tm = u_ref.shape[0]
    u = jax.nn.gelu(u_ref[...])
    v = jax.nn.gelu(v_ref[...])
    vc = v - jnp.mean(v, axis=-1, keepdims=True)
    var = jnp.mean(vc * vc, axis=-1, keepdims=True)
    vln = vc * lax.rsqrt(var + EPS) * lng_ref[...] + lnb_ref[...]
    vln_ref[...] = vln

    row = lax.broadcasted_iota(jnp.int32, (CHUNK, CHUNK), 0)
    col = lax.broadcasted_iota(jnp.int32, (CHUNK, CHUNK), 1)
    causal = row >= col
    vln_bf = vln.astype(BF16)
    chunks = []
    for c in range(tm // CHUNK):
        groups = []
        for g in range(A_GROUPS):
            w = jnp.where(causal, ws_ref[g], 0.0).astype(BF16)
            vg = vln_bf[c * CHUNK:(c + 1) * CHUNK, g * A_GROUP_DIM:(g + 1) * A_GROUP_DIM]
            groups.append(jnp.dot(w, vg, preferred_element_type=F32) + bst_ref[:, g:g + 1])
        chunks.append(jnp.concatenate(groups, axis=1))
    y_a = u * jnp.concatenate(chunks, axis=0)

    a = jnp.dot(y_a.astype(BF16), wa_ref[...], preferred_element_type=F32)
    bproj = jnp.dot(yb_ref[...].astype(BF16), wb_ref[...], preferred_element_type=F32)
    g_a = jnp.concatenate([ga0_ref[...], ga1_ref[...]], axis=1)
    g_b = jnp.concatenate([gb0_ref[...], gb1_ref[...]], axis=1)
    merged = jax.nn.sigmoid(g_a) * a + jax.nn.sigmoid(g_b) * bproj
    x1_ref[...] = x_ref[...] + jnp.dot(merged.astype(BF16), wo_ref[...], preferred_element_type=F32)


def _mix(z, y_b, x, lnv_g, lnv_b, w_s, b_s_t, w_a, w_b, w_o, tm):
    t = x.shape[0]
    wide = lambda blk: pl.BlockSpec((tm, D_MODEL), lambda i: (i, blk))
    narrow = lambda blk: pl.BlockSpec((tm, COL_BLK), lambda i: (i, blk))
    full = lambda a: pl.BlockSpec(a.shape, lambda i: (0,) * a.ndim)
    weights = (lnv_g, lnv_b, w_s, b_s_t, w_a, w_b, w_o)
    return pl.pallas_call(
        _mix_kernel,
        grid=(t // tm,),
        in_specs=([wide(0), wide(1), narrow(GA_BLK), narrow(GA_BLK + 1), narrow(GB_BLK), narrow(GB_BLK + 1)]
                  + [narrow(0), wide(0)] + [full(a) for a in weights]),
        out_specs=[wide(0), wide(0)],
        out_shape=[jax.ShapeDtypeStruct((t, D_MODEL), F32)] * 2,
        compiler_params=_params("parallel"),
        name="mix",
    )(z, z, z, z, z, z, y_b, x, *weights)


def _top_rows(s, k):
    n, cols = s.shape
    row = lax.broadcasted_iota(jnp.int32, (n, cols), 0)
    rank_row = lax.broadcasted_iota(jnp.int32, (k, cols), 0)

    def body(r, carry):
        work, rank, vals = carry
        m = jnp.max(work, axis=0, keepdims=True)
        first = jnp.min(jnp.where(work == m, row, n), axis=0, keepdims=True)
        sel = row == first
        return (jnp.where(sel, -jnp.inf, work), jnp.where(sel, r, rank), jnp.where(rank_row == r, m, vals))

    init = (s, jnp.full((n, cols), k, jnp.int32), jnp.zeros((k, cols), F32))
    _, rank, vals = lax.fori_loop(0, k, body, init)
    return rank, vals


def _select_exact(s0, s1):
    k = PEER_TOPK
    rank0, vals0 = _top_rows(s0, k)
    rank1, vals1 = _top_rows(s1, k)
    cand = jnp.concatenate([vals0[a:a + 1, :] + vals1 for a in range(k)], axis=0)
    pair_rank, pair_vals = _top_rows(cand, k)
    taken = jnp.where(pair_rank < k, 1.0, 0.0)
    z = jnp.sum(taken * jnp.exp(cand - pair_vals[0:1, :]), axis=0, keepdims=True)
    n_b = [jnp.sum(taken[a * k:(a + 1) * k, :], axis=0, keepdims=True) for a in range(k)]
    nb_key = jnp.zeros(s0.shape, F32)
    for a in range(k):
        nb_key = jnp.where(rank0 == a, n_b[a], nb_key)
    e0 = jnp.where(rank0 < k, jnp.exp(s0 - vals0[0:1, :]), 0.0)
    e1 = jnp.where(rank1 < k, jnp.exp(s1 - vals1[0:1, :]), 0.0)
    return e0 / z, nb_key, e1, rank1.astype(F32)


def _larger_smaller(a, b):
    if a is None:
        return b, None
    if b is None:
        return a, None
    return jnp.maximum(a, b), jnp.minimum(a, b)


def _sort_bitonic(xs):
    xs = list(xs)
    j = len(xs) // 2
    while j >= 1:
        for i in range(len(xs)):
            if i & j == 0:
                xs[i], xs[i | j] = _larger_smaller(xs[i], xs[i | j])
        j //= 2
    return xs


def _sort_desc(xs):
    if len(xs) == 1:
        return list(xs)
    half = len(xs) // 2
    return _sort_bitonic(_sort_desc(xs[:half]) + _sort_desc(xs[half:])[::-1])


def _top_merge(xs, ys):
    n = len(xs)
    return _sort_bitonic([_larger_smaller(xs[i], ys[n - 1 - i])[0] for i in range(n)])


def _merge_keep(xs, ys, keep):
    n = 1
    while n < max(len(xs), len(ys)):
        n *= 2
    xs = list(xs) + [None] * (n - len(xs))
    ys = list(ys) + [None] * (n - len(ys))
    out = _top_merge(xs, ys) if n >= keep else _sort_bitonic(xs + ys[::-1])
    return [v for v in out if v is not None][:keep]


def _sublane_total(v):
    for shift in (1, 2, 4):
        v = v + pltpu.roll(v, shift, 0)
    return v


def _top_values(blocks):
    xs = _sort_desc(blocks)
    for shift in (1, 2, 4):
        xs = _top_merge(xs, [pltpu.roll(v, shift, 0) for v in xs])
    return xs


def _select_distinct(s0, s1):
    k = PEER_TOPK
    sub = s0.shape[0] // k
    x0 = [s0[sub * j:sub * (j + 1), :] for j in range(k)]
    x1 = [s1[sub * j:sub * (j + 1), :] for j in range(k)]
    v0, v1 = _top_values(x0), _top_values(x1)
    cand = [[v0[a] + v1[b] for b in range(k // (a + 1))] for a in range(k)]
    col = [cand[a][0] for a in range(k // 2, k)]
    rest = _merge_keep(_merge_keep(cand[1], cand[2], k),
                       _merge_keep(_merge_keep(cand[3], cand[4], k), _merge_keep(cand[5], cand[6], k), k), k)
    rest = _merge_keep(rest, _merge_keep(cand[7], col, k), k)
    top = _merge_keep(cand[0], rest, k)
    tau = top[k - 1]
    z = functools.reduce(lambda acc, v: acc + jnp.exp(v - top[0]), top[1:], jnp.ones_like(tau))
    n_b = [functools.reduce(lambda acc, c: acc + jnp.where(c >= tau, 1.0, 0.0), row, jnp.zeros_like(tau))
           for row in cand]
    count = lambda xs, v: _sublane_total(functools.reduce(
        lambda acc, x: acc + jnp.where(x >= v, 1.0, 0.0), xs, jnp.zeros_like(v)))
    distinct = jnp.where(jnp.logical_and(jnp.logical_and(count(x0, v0[k - 1]) == k, count(x1, v1[k - 1]) == k),
                                         functools.reduce(jnp.add, n_b) == k), 1.0, 0.0)
    inv_z = 1.0 / z
    gate0, nb_key, gate1, rank1 = [], [], [], []
    for j in range(k):
        nb_j = jnp.zeros_like(tau)
        r_j = jnp.full_like(tau, float(k))
        for a in range(k):
            nb_j = jnp.where(x0[j] == v0[a], n_b[a], nb_j)
            r_j = jnp.where(x1[j] == v1[a], float(a), r_j)
        nb_key.append(nb_j)
        rank1.append(r_j)
        gate0.append(jnp.exp(x0[j] - v0[0]) * inv_z)
        gate1.append(jnp.exp(x1[j] - v1[0]))
    cat = lambda xs: jnp.concatenate(xs, axis=0)
    return cat(gate0), cat(nb_key), cat(gate1), cat(rank1), distinct


def _peer_select_kernel(x_ref, g_ref, wq_ref, sk_ref, ht_ref, a_ref, nb_ref, b_ref, r1_ref, qp_ref):
    h2 = _rms(x_ref[...], g_ref[...])
    ht_ref[...] = h2.T.astype(BF16)
    qp_ref[...] = jnp.dot(h2.astype(BF16), wq_ref[...], preferred_element_type=F32).astype(BF16)

    def head(h, carry):
        s = []
        for p in range(2):
            off = pl.multiple_of((h * 2 + p) * D_HALF, D_HALF)
            s.append(lax.dot_general(sk_ref[p], qp_ref[:, pl.ds(off, D_HALF)], (((1,), (1,)), ((), ())),
                                     preferred_element_type=F32))

        def write(gate0, nb_key, gate1, rank1):
            a_ref[h] = gate0
            nb_ref[h] = nb_key
            b_ref[h] = gate1.astype(BF16)
            r1_ref[h] = rank1.astype(BF16)

        *fast, distinct = _select_distinct(s[0], s[1])
        all_distinct = jnp.min(distinct) > 0.5

        @pl.when(all_distinct)
        def _():
            write(*fast)

        @pl.when(jnp.logical_not(all_distinct))
        def _():
            write(*_select_exact(s[0], s[1]))

        return carry

    lax.fori_loop(0, PEER_HEADS, head, 0)


def _peer_select(x1, g2, wq_bf, sk_bf, tb):
    t = x1.shape[0]
    sel_spec = pl.BlockSpec((PEER_HEADS, N_KEYS, tb), lambda i: (0, 0, i))
    sel_shape = lambda dtype: jax.ShapeDtypeStruct((PEER_HEADS, N_KEYS, t), dtype)
    return pl.pallas_call(
        _peer_select_kernel,
        grid=(t // tb,),
        in_specs=[
            pl.BlockSpec((tb, D_MODEL), lambda i: (i, 0)),
            pl.BlockSpec((1, D_MODEL), lambda i: (0, 0)),
            pl.BlockSpec(wq_bf.shape, lambda i: (0, 0)),
            pl.BlockSpec(sk_bf.shape, lambda i: (0, 0, 0)),
        ],
        out_specs=[pl.BlockSpec((D_MODEL, tb), lambda i: (0, i))] + [sel_spec] * 4,
        out_shape=[jax.ShapeDtypeStruct((D_MODEL, t), BF16), sel_shape(F32), sel_shape(F32),
                   sel_shape(BF16), sel_shape(BF16)],
        scratch_shapes=[pltpu.VMEM((tb, PEER_HEADS * D_QUERY), BF16)],
        compiler_params=_params("parallel"),
        name="peer_select",
    )(x1, g2, wq_bf, sk_bf)


def _peer_dense_kernel(ht_ref, u_ref, vt_ref, a_ref, nb_ref, b_ref, r1_ref, x1_ref, gf_ref,
                       y_ref, acc_ref, gact_ref):
    c = pl.program_id(1)

    @pl.when(c == 0)
    def _():
        acc_ref[...] = jnp.zeros_like(acc_ref)

    tb = ht_ref.shape[1]
    act = _gelu_tanh(jnp.dot(u_ref[...], ht_ref[...], preferred_element_type=F32)).astype(BF16)
    tiles = N_KEYS // BF16_ROWS
    for il in range(u_ref.shape[0] // N_KEYS):
        rows = slice(il * N_KEYS, (il + 1) * N_KEYS)
        gate = None
        for h in range(PEER_HEADS):
            row = lambda ref: jnp.broadcast_to(ref[h, il:il + 1, :], (BF16_ROWS, tb)).astype(BF16)[None]
            rank1 = r1_ref[h].reshape(tiles, BF16_ROWS, tb)
            gate1 = b_ref[h].reshape(tiles, BF16_ROWS, tb)
            term = jnp.where(rank1 < row(nb_ref), row(a_ref) * gate1, jnp.zeros_like(gate1))
            gate = term if gate is None else gate + term
        gact_ref[rows, :] = gate.reshape(N_KEYS, tb) * act[rows, :]
    acc_ref[...] += jnp.dot(vt_ref[...], gact_ref[...], preferred_element_type=F32)

    @pl.when(c == pl.num_programs(1) - 1)
    def _():
        y_ref[...] = _rms(x1_ref[...] + acc_ref[...].T, gf_ref[...])


def _peer_dense(ht, u_bf, vt_bf, sel, x1, gf, tb, ec):
    t = x1.shape[0]
    a, nb, b, r1 = sel
    key_rows = ec // N_KEYS
    return pl.pallas_call(
        _peer_dense_kernel,
        grid=(t // tb, N_EXPERTS // ec),
        in_specs=[
            pl.BlockSpec((D_MODEL, tb), lambda i, c: (0, i)),
            pl.BlockSpec((ec, D_MODEL), lambda i, c: (c, 0)),
            pl.BlockSpec((D_MODEL, ec), lambda i, c: (0, c)),
            pl.BlockSpec((PEER_HEADS, key_rows, tb), lambda i, c: (0, c, i)),
            pl.BlockSpec((PEER_HEADS, key_rows, tb), lambda i, c: (0, c, i)),
            pl.BlockSpec((PEER_HEADS, N_KEYS, tb), lambda i, c: (0, 0, i)),
            pl.BlockSpec((PEER_HEADS, N_KEYS, tb), lambda i, c: (0, 0, i)),
            pl.BlockSpec((tb, D_MODEL), lambda i, c: (i, 0)),
            pl.BlockSpec((1, D_MODEL), lambda i, c: (0, 0)),
        ],
        out_specs=pl.BlockSpec((tb, D_MODEL), lambda i, c: (i, 0)),
        out_shape=jax.ShapeDtypeStruct((t, D_MODEL), F32),
        scratch_shapes=[pltpu.VMEM((D_MODEL, tb), F32), pltpu.VMEM((ec, tb), BF16)],
        compiler_params=_params("parallel", "arbitrary"),
        name="peer_dense",
    )(ht, u_bf, vt_bf, a, nb, b, r1, x1, gf)


def _tile(t, pref):
    return pref if t % pref == 0 else t


def _layer_tail(z, attn, x, wts, chunk_w, chunk_b_t):
    t = x.shape[0]
    x1, vln = _mix(z, attn, x, wts["lnv_g"], wts["lnv_b"], chunk_w, chunk_b_t,
                   wts["w_a"], wts["w_b"], wts["w_o"], _tile(t, 256))
    tb = _tile(t, 256)
    ht, *sel = _peer_select(x1, wts["norm2_g"], wts["peer_wq"], wts["peer_subkeys"], tb)
    tb = _tile(t, 512)
    y = _peer_dense(ht, wts["peer_u"], wts["peer_vt"], sel, x1, wts["final_norm_g"], tb, 1024)
    return y, vln


def kernel(x_prompt, x_sample, cache_k_w128, cache_v_w128, cache_k_w512, cache_v_w512,
           cache_k_w2048, cache_v_w2048, norm1_g, w_in, lnv_g, lnv_b, w_s, b_s, w_a, w_b, w_o,
           norm2_g, peer_wq, peer_subkeys, peer_u, peer_v, final_norm_g):
    depth = w_in.shape[0]
    assert depth == 1, "one layer: the final rmsnorm is fused into the layer's last kernel"
    batch, seq, _ = x_prompt.shape
    dec_batch, dec_seq, _ = x_sample.shape
    tp, ts = batch * seq, dec_batch * dec_seq
    assert seq % (QB * DIL_RATES[-1]) == 0 and dec_seq <= min(DIL_RATES[1:]) and CHUNK % dec_seq == 0
    assert ts % CHUNK == 0 and all(c.shape[2] == w for c, w in zip((cache_k_w128, cache_k_w512, cache_k_w2048), DIL_WINDOWS))
    l = 0
    row = lambda a: a[l].reshape(1, -1)
    wts = {
        "lnv_g": row(lnv_g), "lnv_b": row(lnv_b),
        "w_a": w_a[l].astype(BF16), "w_b": w_b[l].astype(BF16), "w_o": w_o[l].astype(BF16),
        "norm2_g": row(norm2_g), "peer_wq": peer_wq[l].astype(BF16),
        "peer_subkeys": peer_subkeys[l].astype(BF16),
        "peer_u": peer_u[l].astype(BF16), "peer_vt": peer_v[l].T.astype(BF16),
        "final_norm_g": final_norm_g.reshape(1, -1),
    }
    g1 = row(norm1_g)
    w_in_bf = w_in[l].astype(BF16)

    xp = x_prompt.reshape(tp, D_MODEL)
    cos_p, sin_p = _rope_tables(jnp.arange(seq, dtype=jnp.int32))
    zp = _project(xp, g1, w_in_bf, cos_p, sin_p, 1024)
    attn_p = _attn_prompt(zp, batch, seq)
    yp, vln_p = _layer_tail(zp, attn_p, xp, wts, w_s[l], jnp.transpose(b_s[l]))

    xs = x_sample.reshape(ts, D_MODEL)
    pos_s = PAST_LEN + jnp.tile(jnp.arange(dec_seq, dtype=jnp.int32), dec_batch)
    cos_s, sin_s = _rope_tables(pos_s)
    zs = _project(xs, g1, w_in_bf, cos_s, sin_s, ts)
    cache_k = [c[l].reshape(dec_batch, w, D_B) for c, w in zip((cache_k_w128, cache_k_w512, cache_k_w2048), DIL_WINDOWS)]
    cache_v = [c[l].reshape(dec_batch, w, D_B) for c, w in zip((cache_v_w128, cache_v_w512, cache_v_w2048), DIL_WINDOWS)]
    attn_s = _attn_sample(zs, dec_batch, dec_seq, cache_k, cache_v)
    eye = jnp.eye(CHUNK // dec_seq, dtype=F32)
    w_s_blk = jax.vmap(lambda w: jnp.kron(eye, w))(w_s[l][:, :dec_seq, :dec_seq])
    b_s_blk_t = jnp.transpose(jnp.tile(b_s[l][:, :dec_seq], (1, CHUNK // dec_seq)))
    ys, vln_s = _layer_tail(zs, attn_s, xs, wts, w_s_blk, b_s_blk_t)

    def heads(a, b_, rows):
        return a.reshape(1, b_, rows, B_HEADS, HEAD_DIM)

    zp3 = zp.reshape(batch, seq, D_IN)
    zs3 = zs.reshape(dec_batch, dec_seq, D_IN)
    prompt_kv, sample_kv = [], []
    for g, w in enumerate(DIL_WINDOWS):
        keep = min(w, seq)
        for blk, cache in ((K_BLK, cache_k[g]), (V_BLK, cache_v[g])):
            cols = slice((blk + g) * COL_BLK, (blk + g + 1) * COL_BLK)
            prompt_kv.append(heads(zp3[:, seq - keep:, cols], batch, keep))
            rows = jnp.concatenate([cache, zs3[:, :, cols]], axis=1)
            keep_s = min(w, rows.shape[1])
            sample_kv.append(heads(rows[:, rows.shape[1] - keep_s:], dec_batch, keep_s))
    tail = seq - CHUNK * ((seq - 1) // CHUNK)
    prompt_gmlp_v = vln_p.reshape(1, batch, seq, D_A)[:, :, seq - tail:]
    sample_gmlp_v = vln_s.reshape(1, dec_batch, dec_seq, D_A)
    return (yp.reshape(batch, seq, D_MODEL), ys.reshape(dec_batch, dec_seq, D_MODEL),
            *prompt_kv, prompt_gmlp_v, *sample_kv, sample_gmlp_v)
```

```python
import functools

import jax
import jax.numpy as jnp
from jax import lax
from jax.experimental import pallas as pl
from jax.experimental.pallas import tpu as pltpu

F32 = jnp.float32
BF16 = jnp.bfloat16

D_MODEL = 1024
PAST_LEN = 16384
CHUNK = 128
D_A = D_MODEL
A_GROUPS = 8
A_GROUP_DIM = D_A // A_GROUPS
DIL_WINDOWS = (128, 512, 2048)
DIL_RATES = (1, 4, 16)
N_DIL = 3
N_BACK = 128
B_HEADS = 8
HEAD_DIM = 64
D_B = B_HEADS * HEAD_DIM
ROPE_THETA = 10000.0
N_KEYS = 128
N_EXPERTS = N_KEYS * N_KEYS
PEER_HEADS = 8
PEER_TOPK = 16
D_QUERY = 256
D_HALF = D_QUERY // 2
EPS = 1e-6
OFF_QKV = 2 * D_A
OFF_GATE = OFF_QKV + 3 * N_DIL * D_B
D_IN = OFF_GATE + 2 * D_MODEL

COL_BLK = 512
N_COL_BLK = D_IN // COL_BLK
Q_BLK = OFF_QKV // COL_BLK
K_BLK = Q_BLK + N_DIL
V_BLK = K_BLK + N_DIL
GA_BLK = OFF_GATE // COL_BLK
GB_BLK = GA_BLK + D_MODEL // COL_BLK

LANES = 128
BF16_ROWS = 16
QB = 128
SUPER = QB * DIL_RATES[-1]
ATTN_UNROLL = 4
DOUBLE_BUFFER_MAX_WINDOW = 512
NEG = float(jnp.finfo(jnp.float32).min)
VMEM_LIMIT = 56 * 1024 * 1024


def _params(*sem):
    return pltpu.CompilerParams(dimension_semantics=sem, vmem_limit_bytes=VMEM_LIMIT)


def _rms(x, g):
    return x * lax.rsqrt(jnp.mean(x * x, axis=-1, keepdims=True) + EPS) * g


def _gelu_tanh(x):
    c = -2.0 * (2.0 / jnp.pi) ** 0.5 * 1.4426950408889634
    return x / (1.0 + jnp.exp2(x * (c + (c * 0.044715) * (x * x))))


def _project_kernel(x_ref, g_ref, w_ref, cos_ref, sin_ref, z_ref, h_ref):
    j = pl.program_id(1)

    @pl.when(j == 0)
    def _():
        h_ref[...] = _rms(x_ref[...], g_ref[...]).astype(BF16)

    acc = jnp.dot(h_ref[...], w_ref[...], preferred_element_type=F32)
    is_rope = jnp.logical_and(j >= Q_BLK, j < V_BLK)

    @pl.when(is_rope)
    def _():
        lane = lax.broadcasted_iota(jnp.int32, acc.shape, 1)
        first_half = (lane % HEAD_DIM) < (HEAD_DIM // 2)
        partner = jnp.where(first_half,
                            pltpu.roll(acc, COL_BLK - HEAD_DIM // 2, 1),
                            pltpu.roll(acc, HEAD_DIM // 2, 1))
        z_ref[...] = acc * cos_ref[...] + partner * sin_ref[...]

    @pl.when(jnp.logical_not(is_rope))
    def _():
        z_ref[...] = acc


def _project(x, g, w_bf, cos, sin, tm):
    t = x.shape[0]
    n_pos_blk = cos.shape[0] // tm
    return pl.pallas_call(
        _project_kernel,
        grid=(t // tm, N_COL_BLK),
        in_specs=[
            pl.BlockSpec((tm, D_MODEL), lambda i, j: (i, 0)),
            pl.BlockSpec((1, D_MODEL), lambda i, j: (0, 0)),
            pl.BlockSpec((D_MODEL, COL_BLK), lambda i, j: (0, j)),
            pl.BlockSpec((tm, COL_BLK), lambda i, j: (i % n_pos_blk, 0)),
            pl.BlockSpec((tm, COL_BLK), lambda i, j: (i % n_pos_blk, 0)),
        ],
        out_specs=pl.BlockSpec((tm, COL_BLK), lambda i, j: (i, j)),
        out_shape=jax.ShapeDtypeStruct((t, D_IN), F32),
        scratch_shapes=[pltpu.VMEM((tm, D_MODEL), BF16)],
        compiler_params=_params("parallel", "arbitrary"),
        name="project",
    )(x, g, w_bf, cos, sin)


def _rope_tables(pos):
    half = HEAD_DIM // 2
    inv = 1.0 / (ROPE_THETA ** (jnp.arange(half, dtype=F32) * (2.0 / HEAD_DIM)))
    ang = pos.astype(F32)[:, None] * inv[None, :]
    cos, sin = jnp.cos(ang), jnp.sin(ang)
    cos_h = jnp.concatenate([cos, cos], axis=1)
    sin_h = jnp.concatenate([-sin, sin], axis=1)
    reps = COL_BLK // HEAD_DIM
    return jnp.tile(cos_h, (1, reps)), jnp.tile(sin_h, (1, reps))


def _softmax_rows(s):
    mx = jnp.max(s, axis=-1, keepdims=True)
    p = jnp.exp(s - mx)
    den = jnp.sum(p, axis=-1, keepdims=True)
    return p / den, mx + jnp.log(den)


def _combine_groups(outs, lses):
    mx = functools.reduce(jnp.maximum, lses)
    es = [jnp.exp(l - mx) for l in lses]
    den = functools.reduce(jnp.add, es)
    return functools.reduce(jnp.add, [(e / den) * o for e, o in zip(es, outs)])


def _attn_head_pair(q, kp, kc, vp, vc, valid):
    lane = lax.broadcasted_iota(jnp.int32, (QB, LANES), 1)
    low_head = lane < HEAD_DIM
    q2 = q * (HEAD_DIM ** -0.5)
    k2 = jnp.concatenate([kp, kc], axis=0).astype(BF16)
    v2 = jnp.concatenate([vp, vc], axis=0).astype(BF16)
    o_pair, lse_pair = [], []
    for head_mask in (low_head, jnp.logical_not(low_head)):
        qm = jnp.where(head_mask, q2, 0.0).astype(BF16)
        s = lax.dot_general(qm, k2, (((1,), (1,)), ((), ())), preferred_element_type=F32)
        p, lse = _softmax_rows(jnp.where(valid, s, NEG))
        o_pair.append(jnp.dot(p.astype(BF16), v2, preferred_element_type=F32))
        lse_pair.append(lse)
    return jnp.where(low_head, o_pair[0], o_pair[1]), jnp.where(low_head, lse_pair[0], lse_pair[1])


def _attn_prompt_kernel(*refs):
    n_g = N_DIL
    q_refs, kc_refs, vc_refs = refs[0:n_g], refs[n_g:2 * n_g], refs[2 * n_g:3 * n_g]
    kp_refs, vp_refs = refs[3 * n_g:4 * n_g], refs[4 * n_g:5 * n_g]
    y_ref, o_scr, lse_scr = refs[5 * n_g:]
    n = pl.program_id(1)
    row = lax.broadcasted_iota(jnp.int32, (QB, 2 * QB), 0)
    col = lax.broadcasted_iota(jnp.int32, (QB, 2 * QB), 1)
    back = QB + row - col
    in_band = jnp.logical_and(back >= 0, back <= N_BACK)
    first_col = jnp.where(n > 0, 0, QB)
    in_band_first = jnp.logical_and(in_band, col >= first_col)

    for g in range(n_g):
        r = DIL_RATES[g]
        span = QB * r

        def rows_at(start, r=r):
            return pl.ds(start, QB, stride=r) if r > 1 else pl.ds(start, QB)

        def store(g, rows, result):
            o_scr[g, rows, :], lse_scr[g, rows, :] = result

        def first_item(rho, carry, g=g, rows_at=rows_at):
            rows = rows_at(rho)
            store(g, rows, _attn_head_pair(q_refs[g][rows, :], kp_refs[g][rows, :], kc_refs[g][rows, :],
                                           vp_refs[g][rows, :], vc_refs[g][rows, :], in_band_first))
            return carry

        def inner_item(i, carry, g=g, r=r, span=span, rows_at=rows_at):
            start = (1 + i // r) * span + i % r
            rows, prev = rows_at(start), rows_at(start - span)
            store(g, rows, _attn_head_pair(q_refs[g][rows, :], kc_refs[g][prev, :], kc_refs[g][rows, :],
                                           vc_refs[g][prev, :], vc_refs[g][rows, :], in_band))
            return carry

        n_inner = (SUPER // span - 1) * r
        lax.fori_loop(0, r, first_item, 0, unroll=min(r, ATTN_UNROLL))
        if n_inner:
            lax.fori_loop(0, n_inner, inner_item, 0,
                          unroll=max(u for u in range(1, ATTN_UNROLL + 2) if n_inner % u == 0))

    y_ref[...] = _combine_groups([o_scr[g] for g in range(n_g)], [lse_scr[g] for g in range(n_g)])


def _attn_prompt(z, batch, seq):
    t = batch * seq
    n_super = seq // SUPER
    lane_blks = COL_BLK // LANES

    def cur(col_blk):
        return pl.BlockSpec((SUPER, LANES), lambda b, n, hp: (b * n_super + n, col_blk * lane_blks + hp))

    def prev(col_blk, r):
        span = QB * r
        per_super, per_seq = SUPER // span, seq // span
        return pl.BlockSpec((span, LANES), lambda b, n, hp: (b * per_seq + jnp.maximum(n * per_super - 1, 0),
                                                             col_blk * lane_blks + hp))

    groups = range(N_DIL)
    return pl.pallas_call(
        _attn_prompt_kernel,
        grid=(batch, n_super, D_B // LANES),
        in_specs=([cur(Q_BLK + g) for g in groups] + [cur(K_BLK + g) for g in groups]
                  + [cur(V_BLK + g) for g in groups]
                  + [prev(K_BLK + g, DIL_RATES[g]) for g in groups]
                  + [prev(V_BLK + g, DIL_RATES[g]) for g in groups]),
        out_specs=pl.BlockSpec((SUPER, LANES), lambda b, n, hp: (b * n_super + n, hp)),
        out_shape=jax.ShapeDtypeStruct((t, D_B), F32),
        scratch_shapes=[pltpu.VMEM((N_DIL, SUPER, LANES), F32)] * 2,
        compiler_params=_params("parallel", "parallel", "arbitrary"),
        name="attn_prompt",
    )(*([z] * (5 * N_DIL)))


def _attn_sample_kernel(dec_seq, *refs):
    n_g = N_DIL
    q_refs, kn_refs, vn_refs = refs[0:n_g], refs[n_g:2 * n_g], refs[2 * n_g:3 * n_g]
    kc_refs, vc_refs = refs[3 * n_g:4 * n_g], refs[4 * n_g:5 * n_g]
    y_ref = refs[5 * n_g]
    ko_refs, vo_refs = refs[5 * n_g + 1:6 * n_g + 1], refs[6 * n_g + 1:7 * n_g + 1]
    sems = refs[7 * n_g + 1]
    b = pl.program_id(0)

    copies = []
    for g, w in enumerate(DIL_WINDOWS):
        for cache, new, out in ((kc_refs[g], kn_refs[g], ko_refs[g]), (vc_refs[g], vn_refs[g], vo_refs[g])):
            kept = w - dec_seq
            copies.append(pltpu.make_async_copy(cache.at[0, pl.ds(dec_seq, kept)], out.at[b, pl.ds(0, kept)],
                                                sems.at[len(copies)]))
            copies.append(pltpu.make_async_copy(new.at[0], out.at[b, pl.ds(kept, dec_seq)],
                                                sems.at[len(copies)]))
    for cp in copies:
        cp.start()

    rows = dec_seq * B_HEADS
    scale = HEAD_DIM ** -0.5
    flat = lambda ref: ref[0].reshape(-1, HEAD_DIM)
    outs, lses = [], []
    for g in range(n_g):
        r, w = DIL_RATES[g], DIL_WINDOWS[g]
        q = (flat(q_refs[g]) * scale).astype(BF16)

        def scores(keys, first_pos, r=r, w=w, q=q):
            s = lax.dot_general(q, keys, (((1,), (1,)), ((), ())), preferred_element_type=F32)
            row = lax.broadcasted_iota(jnp.int32, s.shape, 0)
            col = lax.broadcasted_iota(jnp.int32, s.shape, 1)
            back = (w + row // B_HEADS) - (first_pos + col // B_HEADS)
            valid = jnp.logical_and(jnp.logical_and(row % B_HEADS == col % B_HEADS, back >= 0),
                                    jnp.logical_and(jnp.bitwise_and(back, r - 1) == 0, back <= N_BACK * r))
            return jnp.where(valid, s, NEG)

        s_c = scores(flat(kc_refs[g]).astype(BF16), 0)
        s_n = scores(flat(kn_refs[g]).astype(BF16), w)
        mx = jnp.maximum(jnp.max(s_c, axis=-1, keepdims=True), jnp.max(s_n, axis=-1, keepdims=True))
        p_c, p_n = jnp.exp(s_c - mx), jnp.exp(s_n - mx)
        den = jnp.sum(p_c, axis=-1, keepdims=True) + jnp.sum(p_n, axis=-1, keepdims=True)
        outs.append(jnp.dot((p_c / den).astype(BF16), flat(vc_refs[g]).astype(BF16), preferred_element_type=F32)
                    + jnp.dot((p_n / den).astype(BF16), flat(vn_refs[g]).astype(BF16),
                              preferred_element_type=F32))
        lses.append(mx + jnp.log(den))
    y_ref[0] = _combine_groups(outs, lses).reshape(dec_seq, B_HEADS, HEAD_DIM)

    for cp in copies:
        cp.wait()


def _attn_sample(z, dec_batch, dec_seq, cache_k, cache_v):
    t = dec_batch * dec_seq
    z4 = z.reshape(dec_batch, dec_seq, D_IN // HEAD_DIM, HEAD_DIM)
    head_blk = (1, dec_seq, B_HEADS, HEAD_DIM)

    def new_spec(col_blk):
        return pl.BlockSpec(head_blk, lambda b: (b, 0, col_blk, 0))

    def cache_spec(w):
        mode = dict(pipeline_mode=pl.Buffered(1)) if w > DOUBLE_BUFFER_MAX_WINDOW else {}
        return pl.BlockSpec((1, w, B_HEADS, HEAD_DIM), lambda b: (b, 0, 0, 0), **mode)

    groups = range(N_DIL)
    cache_shapes = [jax.ShapeDtypeStruct(c.shape, F32) for c in list(cache_k) + list(cache_v)]
    y, *rolled = pl.pallas_call(
        functools.partial(_attn_sample_kernel, dec_seq),
        grid=(dec_batch,),
        in_specs=([new_spec(Q_BLK + g) for g in groups] + [new_spec(K_BLK + g) for g in groups]
                  + [new_spec(V_BLK + g) for g in groups] + [cache_spec(w) for w in DIL_WINDOWS] * 2),
        out_specs=[pl.BlockSpec(head_blk, lambda b: (b, 0, 0, 0))] + [pl.BlockSpec(memory_space=pl.ANY)] * (2 * N_DIL),
        out_shape=[jax.ShapeDtypeStruct((dec_batch, dec_seq, B_HEADS, HEAD_DIM), F32)] + cache_shapes,
        scratch_shapes=[pltpu.SemaphoreType.DMA((4 * N_DIL,))],
        compiler_params=_params("arbitrary"),
        name="attn_sample",
    )(*([z4] * (3 * N_DIL) + list(cache_k) + list(cache_v)))
    return y.reshape(t, D_B), rolled[:N_DIL], rolled[N_DIL:]


def _mix_kernel(u_ref, v_ref, ga0_ref, ga1_ref, gb0_ref, gb1_ref,
                yb_ref, x_ref,
                lng_ref, lnb_ref, ws_ref, bst_ref, wa_ref, wb_ref, wo_ref,
                x1_ref, vln_ref):
    tm = u_ref.shape[0]
    u = jax.nn.gelu(u_ref[...])
    v = jax.nn.gelu(v_ref[...])
    vc = v - jnp.mean(v, axis=-1, keepdims=True)
    var = jnp.mean(vc * vc, axis=-1, keepdims=True)
    vln = vc * lax.rsqrt(var + EPS) * lng_ref[...] + lnb_ref[...]
    vln_ref[...] = vln

    row = lax.broadcasted_iota(jnp.int32, (CHUNK, CHUNK), 0)
    col = lax.broadcasted_iota(jnp.int32, (CHUNK, CHUNK), 1)
    causal = row >= col
    vln_bf = vln.astype(BF16)
    chunks = []
    for c in range(tm // CHUNK):
        groups = []
        for g in range(A_GROUPS):
            w = jnp.where(causal, ws_ref[g], 0.0).astype(BF16)
            vg = vln_bf[c * CHUNK:(c + 1) * CHUNK, g * A_GROUP_DIM:(g + 1) * A_GROUP_DIM]
            groups.append(jnp.dot(w, vg, preferred_element_type=F32) + bst_ref[:, g:g + 1])
        chunks.append(jnp.concatenate(groups, axis=1))
    y_a = u * jnp.concatenate(chunks, axis=0)

    a = jnp.dot(y_a.astype(BF16), wa_ref[...], preferred_element_type=F32)
    bproj = jnp.dot(yb_ref[...].astype(BF16), wb_ref[...], preferred_element_type=F32)
    g_a = jnp.concatenate([ga0_ref[...], ga1_ref[...]], axis=1)
    g_b = jnp.concatenate([gb0_ref[...], gb1_ref[...]], axis=1)
    merged = jax.nn.sigmoid(g_a) * a + jax.nn.sigmoid(g_b) * bproj
    x1_ref[...] = x_ref[...] + jnp.dot(merged.astype(BF16), wo_ref[...], preferred_element_type=F32)


def _mix(z, y_b, x, lnv_g, lnv_b, w_s, b_s_t, w_a, w_b, w_o, tm):
    t = x.shape[0]
    wide = lambda blk: pl.BlockSpec((tm, D_MODEL), lambda i: (i, blk))
    narrow = lambda blk: pl.BlockSpec((tm, COL_BLK), lambda i: (i, blk))
    full = lambda a: pl.BlockSpec(a.shape, lambda i: (0,) * a.ndim)
    weights = (lnv_g, lnv_b, w_s, b_s_t, w_a, w_b, w_o)
    return pl.pallas_call(
        _mix_kernel,
        grid=(t // tm,),
        in_specs=([wide(0), wide(1), narrow(GA_BLK), narrow(GA_BLK + 1), narrow(GB_BLK), narrow(GB_BLK + 1)]
                  + [narrow(0), wide(0)] + [full(a) for a in weights]),
        out_specs=[wide(0), wide(0)],
        out_shape=[jax.ShapeDtypeStruct((t, D_MODEL), F32)] * 2,
        compiler_params=_params("parallel"),
        name="mix",
    )(z, z, z, z, z, z, y_b, x, *weights)


def _top_rows(s, k):
    n, cols = s.shape
    row = lax.broadcasted_iota(jnp.int32, (n, cols), 0)
    rank_row = lax.broadcasted_iota(jnp.int32, (k, cols), 0)

    def body(r, carry):
        work, rank, vals = carry
        m = jnp.max(work, axis=0, keepdims=True)
        first = jnp.min(jnp.where(work == m, row, n), axis=0, keepdims=True)
        sel = row == first
        return (jnp.where(sel, -jnp.inf, work), jnp.where(sel, r, rank), jnp.where(rank_row == r, m, vals))

    init = (s, jnp.full((n, cols), k, jnp.int32), jnp.zeros((k, cols), F32))
    _, rank, vals = lax.fori_loop(0, k, body, init)
    return rank, vals


def _select_exact(s0, s1):
    k = PEER_TOPK
    rank0, vals0 = _top_rows(s0, k)
    rank1, vals1 = _top_rows(s1, k)
    cand = jnp.concatenate([vals0[a:a + 1, :] + vals1 for a in range(k)], axis=0)
    pair_rank, pair_vals = _top_rows(cand, k)
    taken = jnp.where(pair_rank < k, 1.0, 0.0)
    z = jnp.sum(taken * jnp.exp(cand - pair_vals[0:1, :]), axis=0, keepdims=True)
    n_b = [jnp.sum(taken[a * k:(a + 1) * k, :], axis=0, keepdims=True) for a in range(k)]
    nb_key = jnp.zeros(s0.shape, F32)
    for a in range(k):
        nb_key = jnp.where(rank0 == a, n_b[a], nb_key)
    e0 = jnp.where(rank0 < k, jnp.exp(s0 - vals0[0:1, :]), 0.0)
    e1 = jnp.where(rank1 < k, jnp.exp(s1 - vals1[0:1, :]), 0.0)
    return e0 / z, nb_key, e1, rank1.astype(F32)


def _larger_smaller(a, b):
    if a is None:
        return b, None
    if b is None:
        return a, None
    return jnp.maximum(a, b), jnp.minimum(a, b)


def _sort_bitonic(xs):
    xs = list(xs)
    j = len(xs) // 2
    while j >= 1:
        for i in range(len(xs)):
            if i & j == 0:
                xs[i], xs[i | j] = _larger_smaller(xs[i], xs[i | j])
        j //= 2
    return xs


def _sort_desc(xs):
    if len(xs) == 1:
        return list(xs)
    half = len(xs) // 2
    return _sort_bitonic(_sort_desc(xs[:half]) + _sort_desc(xs[half:])[::-1])


def _top_merge(xs, ys):
    n = len(xs)
    return _sort_bitonic([_larger_smaller(xs[i], ys[n - 1 - i])[0] for i in range(n)])


def _merge_keep(xs, ys, keep):
    n = 1
    while n < max(len(xs), len(ys)):
        n *= 2
    xs = list(xs) + [None] * (n - len(xs))
    ys = list(ys) + [None] * (n - len(ys))
    out = _top_merge(xs, ys) if n >= keep else _sort_bitonic(xs + ys[::-1])
    return [v for v in out if v is not None][:keep]


def _sublane_total(v):
    for shift in (1, 2, 4):
        v = v + pltpu.roll(v, shift, 0)
    return v


def _top_values(blocks):
    xs = _sort_desc(blocks)
    for shift in (1, 2, 4):
        xs = _top_merge(xs, [pltpu.roll(v, shift, 0) for v in xs])
    return xs


def _select_distinct(s0, s1):
    k = PEER_TOPK
    sub = s0.shape[0] // k
    x0 = [s0[sub * j:sub * (j + 1), :] for j in range(k)]
    x1 = [s1[sub * j:sub * (j + 1), :] for j in range(k)]
    v0, v1 = _top_values(x0), _top_values(x1)
    cand = [[v0[a] + v1[b] for b in range(k // (a + 1))] for a in range(k)]
    col = [cand[a][0] for a in range(k // 2, k)]
    rest = _merge_keep(_merge_keep(cand[1], cand[2], k),
                       _merge_keep(_merge_keep(cand[3], cand[4], k), _merge_keep(cand[5], cand[6], k), k), k)
    rest = _merge_keep(rest, _merge_keep(cand[7], col, k), k)
    top = _merge_keep(cand[0], rest, k)
    tau = top[k - 1]
    z = functools.reduce(lambda acc, v: acc + jnp.exp(v - top[0]), top[1:], jnp.ones_like(tau))
    n_b = [functools.reduce(lambda acc, c: acc + jnp.where(c >= tau, 1.0, 0.0), row, jnp.zeros_like(tau))
           for row in cand]
    count = lambda xs, v: _sublane_total(functools.reduce(
        lambda acc, x: acc + jnp.where(x >= v, 1.0, 0.0), xs, jnp.zeros_like(v)))
    distinct = jnp.where(jnp.logical_and(jnp.logical_and(count(x0, v0[k - 1]) == k, count(x1, v1[k - 1]) == k),
                                         functools.reduce(jnp.add, n_b) == k), 1.0, 0.0)
    inv_z = 1.0 / z
    gate0, nb_key, gate1, rank1 = [], [], [], []
    for j in range(k):
        nb_j = jnp.zeros_like(tau)
        r_j = jnp.full_like(tau, float(k))
        for a in range(k):
            nb_j = jnp.where(x0[j] == v0[a], n_b[a], nb_j)
            r_j = jnp.where(x1[j] == v1[a], float(a), r_j)
        nb_key.append(nb_j)
        rank1.append(r_j)
        gate0.append(jnp.exp(x0[j] - v0[0]) * inv_z)
        gate1.append(jnp.exp(x1[j] - v1[0]))
    cat = lambda xs: jnp.concatenate(xs, axis=0)
    return cat(gate0), cat(nb_key), cat(gate1), cat(rank1), distinct


def _peer_select_kernel(x_ref, g_ref, wq_ref, sk_ref, ht_ref, a_ref, nb_ref, b_ref, r1_ref, qp_ref):
    h2 = _rms(x_ref[...], g_ref[...])
    ht_ref[...] = h2.T.astype(BF16)
    qp_ref[...] = jnp.dot(h2.astype(BF16), wq_ref[...], preferred_element_type=F32).astype(BF16)

    def head(h, carry):
        s = []
        for p in range(2):
            off = pl.multiple_of((h * 2 + p) * D_HALF, D_HALF)
            s.append(lax.dot_general(sk_ref[p], qp_ref[:, pl.ds(off, D_HALF)], (((1,), (1,)), ((), ())),
                                     preferred_element_type=F32))

        def write(gate0, nb_key, gate1, rank1):
            a_ref[h] = gate0
            nb_ref[h] = nb_key
            b_ref[h] = gate1.astype(BF16)
            r1_ref[h] = rank1.astype(BF16)

        *fast, distinct = _select_distinct(s[0], s[1])
        all_distinct = jnp.min(distinct) > 0.5

        @pl.when(all_distinct)
        def _():
            write(*fast)

        @pl.when(jnp.logical_not(all_distinct))
        def _():
            write(*_select_exact(s[0], s[1]))

        return carry

    lax.fori_loop(0, PEER_HEADS, head, 0)


def _peer_select(x1, g2, wq_bf, sk_bf, tb):
    t = x1.shape[0]
    sel_spec = pl.BlockSpec((PEER_HEADS, N_KEYS, tb), lambda i: (0, 0, i))
    sel_shape = lambda dtype: jax.ShapeDtypeStruct((PEER_HEADS, N_KEYS, t), dtype)
    return pl.pallas_call(
        _peer_select_kernel,
        grid=(t // tb,),
        in_specs=[
            pl.BlockSpec((tb, D_MODEL), lambda i: (i, 0)),
            pl.BlockSpec((1, D_MODEL), lambda i: (0, 0)),
            pl.BlockSpec(wq_bf.shape, lambda i: (0, 0)),
            pl.BlockSpec(sk_bf.shape, lambda i: (0, 0, 0)),
        ],
        out_specs=[pl.BlockSpec((D_MODEL, tb), lambda i: (0, i))] + [sel_spec] * 4,
        out_shape=[jax.ShapeDtypeStruct((D_MODEL, t), BF16), sel_shape(F32), sel_shape(F32),
                   sel_shape(BF16), sel_shape(BF16)],
        scratch_shapes=[pltpu.VMEM((tb, PEER_HEADS * D_QUERY), BF16)],
        compiler_params=_params("parallel"),
        name="peer_select",
    )(x1, g2, wq_bf, sk_bf)


def _peer_dense_kernel(ht_ref, u_ref, vt_ref, a_ref, nb_ref, b_ref, r1_ref, x1_ref, gf_ref,
                       y_ref, acc_ref, gact_ref):
    c = pl.program_id(1)

    @pl.when(c == 0)
    def _():
        acc_ref[...] = jnp.zeros_like(acc_ref)

    tb = ht_ref.shape[1]
    act = _gelu_tanh(jnp.dot(u_ref[...], ht_ref[...], preferred_element_type=F32)).astype(BF16)
    tiles = N_KEYS // BF16_ROWS
    for il in range(u_ref.shape[0] // N_KEYS):
        rows = slice(il * N_KEYS, (il + 1) * N_KEYS)
        gate = None
        for h in range(PEER_HEADS):
            row = lambda ref: jnp.broadcast_to(ref[h, il:il + 1, :], (BF16_ROWS, tb)).astype(BF16)[None]
            rank1 = r1_ref[h].reshape(tiles, BF16_ROWS, tb)
            gate1 = b_ref[h].reshape(tiles, BF16_ROWS, tb)
            term = jnp.where(rank1 < row(nb_ref), row(a_ref) * gate1, jnp.zeros_like(gate1))
            gate = term if gate is None else gate + term
        gact_ref[rows, :] = gate.reshape(N_KEYS, tb) * act[rows, :]
    acc_ref[...] += jnp.dot(vt_ref[...], gact_ref[...], preferred_element_type=F32)

    @pl.when(c == pl.num_programs(1) - 1)
    def _():
        y_ref[...] = _rms(x1_ref[...] + acc_ref[...].T, gf_ref[...])


def _peer_dense(ht, u_bf, vt_bf, sel, x1, gf, tb, ec):
    t = x1.shape[0]
    a, nb, b, r1 = sel
    key_rows = ec // N_KEYS
    return pl.pallas_call(
        _peer_dense_kernel,
        grid=(t // tb, N_EXPERTS // ec),
        in_specs=[
            pl.BlockSpec((D_MODEL, tb), lambda i, c: (0, i)),
            pl.BlockSpec((ec, D_MODEL), lambda i, c: (c, 0)),
            pl.BlockSpec((D_MODEL, ec), lambda i, c: (0, c)),
            pl.BlockSpec((PEER_HEADS, key_rows, tb), lambda i, c: (0, c, i)),
            pl.BlockSpec((PEER_HEADS, key_rows, tb), lambda i, c: (0, c, i)),
            pl.BlockSpec((PEER_HEADS, N_KEYS, tb), lambda i, c: (0, 0, i)),
            pl.BlockSpec((PEER_HEADS, N_KEYS, tb), lambda i, c: (0, 0, i)),
            pl.BlockSpec((tb, D_MODEL), lambda i, c: (i, 0)),
            pl.BlockSpec((1, D_MODEL), lambda i, c: (0, 0)),
        ],
        out_specs=pl.BlockSpec((tb, D_MODEL), lambda i, c: (i, 0)),
        out_shape=jax.ShapeDtypeStruct((t, D_MODEL), F32),
        scratch_shapes=[pltpu.VMEM((D_MODEL, tb), F32), pltpu.VMEM((ec, tb), BF16)],
        compiler_params=_params("parallel", "arbitrary"),
        name="peer_dense",
    )(ht, u_bf, vt_bf, a, nb, b, r1, x1, gf)


def _token_tiles(t):
    pick = lambda pref: pref if t % pref == 0 else t
    return dict(project=pick(1024), mix=pick(256), select=pick(256), dense=pick(512))


DENSE_EXPERT_CHUNK = 2048


def _layer_tail(z, attn, x, wts, chunk_w, chunk_b_t):
    tiles = _token_tiles(x.shape[0])
    x1, vln = _mix(z, attn, x, wts["lnv_g"], wts["lnv_b"], chunk_w, chunk_b_t,
                   wts["w_a"], wts["w_b"], wts["w_o"], tiles["mix"])
    ht, *sel = _peer_select(x1, wts["norm2_g"], wts["peer_wq"], wts["peer_subkeys"], tiles["select"])
    y = _peer_dense(ht, wts["peer_u"], wts["peer_vt"], sel, x1, wts["final_norm_g"], tiles["dense"],
                    DENSE_EXPERT_CHUNK)
    return y, vln


def kernel(x_prompt, x_sample, cache_k_w128, cache_v_w128, cache_k_w512, cache_v_w512,
           cache_k_w2048, cache_v_w2048, norm1_g, w_in, lnv_g, lnv_b, w_s, b_s, w_a, w_b, w_o,
           norm2_g, peer_wq, peer_subkeys, peer_u, peer_v, final_norm_g):
    depth = w_in.shape[0]
    assert depth == 1, "one layer: the final rmsnorm is fused into the layer's last kernel"
    batch, seq, _ = x_prompt.shape
    dec_batch, dec_seq, _ = x_sample.shape
    tp, ts = batch * seq, dec_batch * dec_seq
    assert seq % (QB * DIL_RATES[-1]) == 0 and dec_seq <= min(DIL_RATES[1:]) and CHUNK % dec_seq == 0
    assert ts % CHUNK == 0 and all(c.shape[2] == w for c, w in zip((cache_k_w128, cache_k_w512, cache_k_w2048), DIL_WINDOWS))
    l = 0
    row = lambda a: a[l].reshape(1, -1)
    wts = {
        "lnv_g": row(lnv_g), "lnv_b": row(lnv_b),
        "w_a": w_a[l].astype(BF16), "w_b": w_b[l].astype(BF16), "w_o": w_o[l].astype(BF16),
        "norm2_g": row(norm2_g), "peer_wq": peer_wq[l].astype(BF16),
        "peer_subkeys": peer_subkeys[l].astype(BF16),
        "peer_u": peer_u[l].astype(BF16), "peer_vt": peer_v[l].T.astype(BF16),
        "final_norm_g": final_norm_g.reshape(1, -1),
    }
    g1 = row(norm1_g)
    w_in_bf = w_in[l].astype(BF16)

    xp = x_prompt.reshape(tp, D_MODEL)
    cos_p, sin_p = _rope_tables(jnp.arange(seq, dtype=jnp.int32))
    zp = _project(xp, g1, w_in_bf, cos_p, sin_p, _token_tiles(tp)["project"])
    attn_p = _attn_prompt(zp, batch, seq)
    yp, vln_p = _layer_tail(zp, attn_p, xp, wts, w_s[l], jnp.transpose(b_s[l]))

    xs = x_sample.reshape(ts, D_MODEL)
    pos_s = PAST_LEN + jnp.tile(jnp.arange(dec_seq, dtype=jnp.int32), dec_batch)
    cos_s, sin_s = _rope_tables(pos_s)
    zs = _project(xs, g1, w_in_bf, cos_s, sin_s, _token_tiles(ts)["project"])
    attn_s, rolled_k, rolled_v = _attn_sample(zs, dec_batch, dec_seq,
                                              [c[l] for c in (cache_k_w128, cache_k_w512, cache_k_w2048)],
                                              [c[l] for c in (cache_v_w128, cache_v_w512, cache_v_w2048)])
    eye = jnp.eye(CHUNK // dec_seq, dtype=F32)
    w_s_blk = jax.vmap(lambda w: jnp.kron(eye, w))(w_s[l][:, :dec_seq, :dec_seq])
    b_s_blk_t = jnp.transpose(jnp.tile(b_s[l][:, :dec_seq], (1, CHUNK // dec_seq)))
    ys, vln_s = _layer_tail(zs, attn_s, xs, wts, w_s_blk, b_s_blk_t)

    def heads(a, b_, rows):
        return a.reshape(1, b_, rows, B_HEADS, HEAD_DIM)

    zp3 = zp.reshape(batch, seq, D_IN)
    prompt_kv, sample_kv = [], []
    for g, w in enumerate(DIL_WINDOWS):
        keep = min(w, seq)
        for blk, rolled in ((K_BLK, rolled_k[g]), (V_BLK, rolled_v[g])):
            cols = slice((blk + g) * COL_BLK, (blk + g + 1) * COL_BLK)
            prompt_kv.append(heads(zp3[:, seq - keep:, cols], batch, keep))
            sample_kv.append(rolled[None])
    tail = seq - CHUNK * ((seq - 1) // CHUNK)
    prompt_gmlp_v = vln_p.reshape(1, batch, seq, D_A)[:, :, seq - tail:]
    sample_gmlp_v = vln_s.reshape(1, dec_batch, dec_seq, D_A)
    return (yp.reshape(batch, seq, D_MODEL), ys.reshape(dec_batch, dec_seq, D_MODEL),
            *prompt_kv, prompt_gmlp_v, *sample_kv, sample_gmlp_v)
```

```python
import functools

import jax
import jax.numpy as jnp
from jax import lax
from jax.experimental import pallas as pl
from jax.experimental.pallas import tpu as pltpu

F32 = jnp.float32
BF16 = jnp.bfloat16

D_MODEL = 1024
PAST_LEN = 16384
CHUNK = 128
D_A = D_MODEL
A_GROUPS = 8
A_GROUP_DIM = D_A // A_GROUPS
DIL_WINDOWS = (128, 512, 2048)
DIL_RATES = (1, 4, 16)
N_DIL = 3
N_BACK = 128
B_HEADS = 8
HEAD_DIM = 64
D_B = B_HEADS * HEAD_DIM
ROPE_THETA = 10000.0
N_KEYS = 128
N_EXPERTS = N_KEYS * N_KEYS
PEER_HEADS = 8
PEER_TOPK = 16
D_QUERY = 256
D_HALF = D_QUERY // 2
EPS = 1e-6
OFF_QKV = 2 * D_A
OFF_GATE = OFF_QKV + 3 * N_DIL * D_B
D_IN = OFF_GATE + 2 * D_MODEL

COL_BLK = 512
N_COL_BLK = D_IN // COL_BLK
Q_BLK = OFF_QKV // COL_BLK
K_BLK = Q_BLK + N_DIL
V_BLK = K_BLK + N_DIL
GA_BLK = OFF_GATE // COL_BLK
GB_BLK = GA_BLK + D_MODEL // COL_BLK

LANES = 128
BF16_ROWS = 16
QB = 128
SUPER = QB * DIL_RATES[-1]
ATTN_UNROLL = 8
NEG = float(jnp.finfo(jnp.float32).min)
VMEM_LIMIT = 56 * 1024 * 1024


def _params(*sem):
    return pltpu.CompilerParams(dimension_semantics=sem, vmem_limit_bytes=VMEM_LIMIT)


def _rms(x, g):
    return x * lax.rsqrt(jnp.mean(x * x, axis=-1, keepdims=True) + EPS) * g


def _gelu_tanh(x):
    c = -2.0 * (2.0 / jnp.pi) ** 0.5 * 1.4426950408889634
    return x / (1.0 + jnp.exp2(x * (c + (c * 0.044715) * (x * x))))


def _project_kernel(x_ref, g_ref, w_ref, cos_ref, sin_ref, z_ref, h_ref):
    j = pl.program_id(1)

    @pl.when(j == 0)
    def _():
        h_ref[...] = _rms(x_ref[...], g_ref[...]).astype(BF16)

    acc = jnp.dot(h_ref[...], w_ref[...], preferred_element_type=F32)
    is_rope = jnp.logical_and(j >= Q_BLK, j < V_BLK)

    @pl.when(is_rope)
    def _():
        lane = lax.broadcasted_iota(jnp.int32, acc.shape, 1)
        first_half = (lane % HEAD_DIM) < (HEAD_DIM // 2)
        partner = jnp.where(first_half,
                            pltpu.roll(acc, COL_BLK - HEAD_DIM // 2, 1),
                            pltpu.roll(acc, HEAD_DIM // 2, 1))
        z_ref[...] = acc * cos_ref[...] + partner * sin_ref[...]

    @pl.when(jnp.logical_not(is_rope))
    def _():
        z_ref[...] = acc


def _project(x, g, w_bf, cos, sin, tm):
    t = x.shape[0]
    n_pos_blk = cos.shape[0] // tm
    return pl.pallas_call(
        _project_kernel,
        grid=(t // tm, N_COL_BLK),
        in_specs=[
            pl.BlockSpec((tm, D_MODEL), lambda i, j: (i, 0)),
            pl.BlockSpec((1, D_MODEL), lambda i, j: (0, 0)),
            pl.BlockSpec((D_MODEL, COL_BLK), lambda i, j: (0, j)),
            pl.BlockSpec((tm, COL_BLK), lambda i, j: (i % n_pos_blk, 0)),
            pl.BlockSpec((tm, COL_BLK), lambda i, j: (i % n_pos_blk, 0)),
        ],
        out_specs=pl.BlockSpec((tm, COL_BLK), lambda i, j: (i, j)),
        out_shape=jax.ShapeDtypeStruct((t, D_IN), F32),
        scratch_shapes=[pltpu.VMEM((tm, D_MODEL), BF16)],
        compiler_params=_params("parallel", "arbitrary"),
        name="project",
    )(x, g, w_bf, cos, sin)


def _rope_tables(pos):
    half = HEAD_DIM // 2
    inv = 1.0 / (ROPE_THETA ** (jnp.arange(half, dtype=F32) * (2.0 / HEAD_DIM)))
    ang = pos.astype(F32)[:, None] * inv[None, :]
    cos, sin = jnp.cos(ang), jnp.sin(ang)
    cos_h = jnp.concatenate([cos, cos], axis=1)
    sin_h = jnp.concatenate([-sin, sin], axis=1)
    reps = COL_BLK // HEAD_DIM
    return jnp.tile(cos_h, (1, reps)), jnp.tile(sin_h, (1, reps))


def _softmax_rows(s):
    mx = jnp.max(s, axis=-1, keepdims=True)
    p = jnp.exp(s - mx)
    den = jnp.sum(p, axis=-1, keepdims=True)
    return p / den, mx + jnp.log(den)


def _combine_groups(outs, lses):
    mx = functools.reduce(jnp.maximum, lses)
    es = [jnp.exp(l - mx) for l in lses]
    den = functools.reduce(jnp.add, es)
    return functools.reduce(jnp.add, [(e / den) * o for e, o in zip(es, outs)])


def _attn_head_pair(q, kp, kc, vp, vc, valid):
    lane = lax.broadcasted_iota(jnp.int32, (QB, LANES), 1)
    low_head = lane < HEAD_DIM
    q2 = q * (HEAD_DIM ** -0.5)
    k2 = jnp.concatenate([kp, kc], axis=0).astype(BF16)
    v2 = jnp.concatenate([vp, vc], axis=0).astype(BF16)
    o_pair, lse_pair = [], []
    for head_mask in (low_head, jnp.logical_not(low_head)):
        qm = jnp.where(head_mask, q2, 0.0).astype(BF16)
        s = lax.dot_general(qm, k2, (((1,), (1,)), ((), ())), preferred_element_type=F32)
        p, lse = _softmax_rows(jnp.where(valid, s, NEG))
        o_pair.append(jnp.dot(p.astype(BF16), v2, preferred_element_type=F32))
        lse_pair.append(lse)
    return jnp.where(low_head, o_pair[0], o_pair[1]), jnp.where(low_head, lse_pair[0], lse_pair[1])


def _attn_prompt_kernel(*refs):
    n_g = N_DIL
    q_refs, kc_refs, vc_refs = refs[0:n_g], refs[n_g:2 * n_g], refs[2 * n_g:3 * n_g]
    kp_refs, vp_refs = refs[3 * n_g:4 * n_g], refs[4 * n_g:5 * n_g]
    y_ref, o_scr, lse_scr = refs[5 * n_g:]
    n = pl.program_id(1)
    row = lax.broadcasted_iota(jnp.int32, (QB, 2 * QB), 0)
    col = lax.broadcasted_iota(jnp.int32, (QB, 2 * QB), 1)
    back = QB + row - col
    in_band = jnp.logical_and(back >= 0, back <= N_BACK)
    first_col = jnp.where(n > 0, 0, QB)
    in_band_first = jnp.logical_and(in_band, col >= first_col)

    for g in range(n_g):
        r = DIL_RATES[g]
        span = QB * r

        def rows_at(start, r=r):
            return pl.ds(start, QB, stride=r) if r > 1 else pl.ds(start, QB)

        def store(g, rows, result):
            o_scr[g, rows, :], lse_scr[g, rows, :] = result

        def first_item(rho, carry, g=g, rows_at=rows_at):
            rows = rows_at(rho)
            store(g, rows, _attn_head_pair(q_refs[g][rows, :], kp_refs[g][rows, :], kc_refs[g][rows, :],
                                           vp_refs[g][rows, :], vc_refs[g][rows, :], in_band_first))
            return carry

        def inner_item(i, carry, g=g, r=r, span=span, rows_at=rows_at):
            start = (1 + i // r) * span + i % r
            rows, prev = rows_at(start), rows_at(start - span)
            store(g, rows, _attn_head_pair(q_refs[g][rows, :], kc_refs[g][prev, :], kc_refs[g][rows, :],
                                           vc_refs[g][prev, :], vc_refs[g][rows, :], in_band))
            return carry

        n_inner = (SUPER // span - 1) * r
        lax.fori_loop(0, r, first_item, 0, unroll=min(r, ATTN_UNROLL))
        if n_inner:
            lax.fori_loop(0, n_inner, inner_item, 0,
                          unroll=max(u for u in range(1, ATTN_UNROLL + 2) if n_inner % u == 0))

    y_ref[...] = _combine_groups([o_scr[g] for g in range(n_g)], [lse_scr[g] for g in range(n_g)])


def _attn_prompt(z, batch, seq):
    t = batch * seq
    n_super = seq // SUPER
    lane_blks = COL_BLK // LANES

    def cur(col_blk):
        return pl.BlockSpec((SUPER, LANES), lambda b, n, hp: (b * n_super + n, col_blk * lane_blks + hp))

    def prev(col_blk, r):
        span = QB * r
        per_super, per_seq = SUPER // span, seq // span
        return pl.BlockSpec((span, LANES), lambda b, n, hp: (b * per_seq + jnp.maximum(n * per_super - 1, 0),
                                                             col_blk * lane_blks + hp))

    groups = range(N_DIL)
    return pl.pallas_call(
        _attn_prompt_kernel,
        grid=(batch, n_super, D_B // LANES),
        in_specs=([cur(Q_BLK + g) for g in groups] + [cur(K_BLK + g) for g in groups]
                  + [cur(V_BLK + g) for g in groups]
                  + [prev(K_BLK + g, DIL_RATES[g]) for g in groups]
                  + [prev(V_BLK + g, DIL_RATES[g]) for g in groups]),
        out_specs=pl.BlockSpec((SUPER, LANES), lambda b, n, hp: (b * n_super + n, hp)),
        out_shape=jax.ShapeDtypeStruct((t, D_B), F32),
        scratch_shapes=[pltpu.VMEM((N_DIL, SUPER, LANES), F32)] * 2,
        compiler_params=_params("parallel", "parallel", "arbitrary"),
        name="attn_prompt",
    )(*([z] * (5 * N_DIL)))


def _attn_sample_kernel(dec_seq, *refs):
    q_refs, kn_refs, vn_refs = refs[0:3], refs[3:6], refs[6:9]
    kc_refs, vc_refs = refs[9:12], refs[12:15]
    y_ref = refs[15]
    outs, lses = [], []
    b = pl.program_id(0)
    rows = dec_seq * B_HEADS
    t_new = kn_refs[0].shape[0]
    row_h = lax.broadcasted_iota(jnp.int32, (B_HEADS, D_B), 0)
    lane_h = lax.broadcasted_iota(jnp.int32, (B_HEADS, D_B), 1) // HEAD_DIM
    head_mask = (row_h == lane_h).astype(F32)
    head_mask_rows = jnp.concatenate([head_mask] * dec_seq, axis=0)
    scale = HEAD_DIM ** -0.5
    for g in range(N_DIL):
        r, w = DIL_RATES[g], DIL_WINDOWS[g]
        q = q_refs[g][0] * scale
        q_bd = jnp.concatenate(
            [jnp.broadcast_to(q[s:s + 1, :], (B_HEADS, D_B)) * head_mask for s in range(dec_seq)],
            axis=0).astype(BF16)
        s_c = lax.dot_general(q_bd, kc_refs[g][0].astype(BF16), (((1,), (1,)), ((), ())),
                              preferred_element_type=F32)
        s_n = lax.dot_general(q_bd, kn_refs[g][...].astype(BF16), (((1,), (1,)), ((), ())),
                              preferred_element_type=F32)
        q_pos_c = lax.broadcasted_iota(jnp.int32, (rows, w), 0) // B_HEADS
        back_c = w + q_pos_c - lax.broadcasted_iota(jnp.int32, (rows, w), 1)
        valid_c = jnp.logical_and(jnp.bitwise_and(back_c, r - 1) == 0, back_c <= N_BACK * r)
        q_pos_n = lax.broadcasted_iota(jnp.int32, (rows, t_new), 0) // B_HEADS
        col_n = lax.broadcasted_iota(jnp.int32, (rows, t_new), 1)
        back_n = q_pos_n - (col_n - b * dec_seq)
        valid_n = jnp.logical_and(jnp.logical_and(back_n >= 0, col_n >= b * dec_seq),
                                  jnp.bitwise_and(back_n, r - 1) == 0)
        s_all = jnp.concatenate([jnp.where(valid_c, s_c, NEG), jnp.where(valid_n, s_n, NEG)], axis=1)
        p, lse = _softmax_rows(s_all)
        p = p.astype(BF16)
        o = (jnp.dot(p[:, :w], vc_refs[g][0].astype(BF16), preferred_element_type=F32)
             + jnp.dot(p[:, w:], vn_refs[g][...].astype(BF16), preferred_element_type=F32))
        outs.append(o)
        lses.append(lse)
    y = _combine_groups(outs, lses)
    y_ref[0] = jnp.sum((y * head_mask_rows).reshape(dec_seq, B_HEADS, D_B), axis=1)


def _attn_sample(z, dec_batch, dec_seq, cache_k, cache_v):
    t = dec_batch * dec_seq
    z3 = z.reshape(dec_batch, dec_seq, D_IN)

    def row_spec(col_blk):
        return pl.BlockSpec((1, dec_seq, COL_BLK), lambda b: (b, 0, col_blk))

    def all_spec(col_blk):
        return pl.BlockSpec((t, COL_BLK), lambda b: (0, col_blk))

    def cache_spec(w):
        return pl.BlockSpec((1, w, D_B), lambda b: (b, 0, 0))

    y = pl.pallas_call(
        functools.partial(_attn_sample_kernel, dec_seq),
        grid=(dec_batch,),
        in_specs=([row_spec(Q_BLK + g) for g in range(N_DIL)]
                  + [all_spec(K_BLK + g) for g in range(N_DIL)]
                  + [all_spec(V_BLK + g) for g in range(N_DIL)]
                  + [cache_spec(w) for w in DIL_WINDOWS] * 2),
        out_specs=pl.BlockSpec((1, dec_seq, D_B), lambda b: (b, 0, 0)),
        out_shape=jax.ShapeDtypeStruct((dec_batch, dec_seq, D_B), F32),
        compiler_params=_params("parallel"),
        name="attn_sample",
    )(*([z3] * N_DIL + [z] * (2 * N_DIL) + list(cache_k) + list(cache_v)))
    return y.reshape(t, D_B)


def _mix_kernel(u_ref, v_ref, ga0_ref, ga1_ref, gb0_ref, gb1_ref,
                yb_ref, x_ref,
                lng_ref, lnb_ref, ws_ref, bst_ref, wa_ref, wb_ref, wo_ref,
                x1_ref, vln_ref):
    tm = u_ref.shape[0]
    u = _gelu_tanh(u_ref[...])
    v = _gelu_tanh(v_ref[...])
    vc = v - jnp.mean(v, axis=-1, keepdims=True)
    var = jnp.mean(vc * vc, axis=-1, keepdims=True)
    vln = vc * lax.rsqrt(var + EPS) * lng_ref[...] + lnb_ref[...]
    vln_ref[...] = vln[tm - CHUNK:, :]

    row = lax.broadcasted_iota(jnp.int32, (CHUNK, CHUNK), 0)
    col = lax.broadcasted_iota(jnp.int32, (CHUNK, CHUNK), 1)
    causal = row >= col
    vln_bf = vln.astype(BF16)
    chunks = []
    for c in range(tm // CHUNK):
        groups = []
        for g in range(A_GROUPS):
            w = jnp.where(causal, ws_ref[g], 0.0).astype(BF16)
            vg = vln_bf[c * CHUNK:(c + 1) * CHUNK, g * A_GROUP_DIM:(g + 1) * A_GROUP_DIM]
            groups.append(jnp.dot(w, vg, preferred_element_type=F32) + bst_ref[:, g:g + 1])
        chunks.append(jnp.concatenate(groups, axis=1))
    y_a = u * jnp.concatenate(chunks, axis=0)

    a = jnp.dot(y_a.astype(BF16), wa_ref[...], preferred_element_type=F32)
    bproj = jnp.dot(yb_ref[...].astype(BF16), wb_ref[...], preferred_element_type=F32)
    g_a = jnp.concatenate([ga0_ref[...], ga1_ref[...]], axis=1)
    g_b = jnp.concatenate([gb0_ref[...], gb1_ref[...]], axis=1)
    merged = jax.nn.sigmoid(g_a) * a + jax.nn.sigmoid(g_b) * bproj
    x1_ref[...] = x_ref[...] + jnp.dot(merged.astype(BF16), wo_ref[...], preferred_element_type=F32)


def _mix(z, y_b, x, lnv_g, lnv_b, w_s, b_s_t, w_a, w_b, w_o, tm, seq_rows):
    t = x.shape[0]
    wide = lambda blk: pl.BlockSpec((tm, D_MODEL), lambda i: (i, blk))
    narrow = lambda blk: pl.BlockSpec((tm, COL_BLK), lambda i: (i, blk))
    full = lambda a: pl.BlockSpec(a.shape, lambda i: (0,) * a.ndim)
    weights = (lnv_g, lnv_b, w_s, b_s_t, w_a, w_b, w_o)
    return pl.pallas_call(
        _mix_kernel,
        grid=(t // tm,),
        in_specs=([wide(0), wide(1), narrow(GA_BLK), narrow(GA_BLK + 1), narrow(GB_BLK), narrow(GB_BLK + 1)]
                  + [narrow(0), wide(0)] + [full(a) for a in weights]),
        out_specs=[wide(0), pl.BlockSpec((CHUNK, D_MODEL), lambda i: (i * tm // seq_rows, 0))],
        out_shape=[jax.ShapeDtypeStruct((t, D_MODEL), F32),
                   jax.ShapeDtypeStruct((t // seq_rows * CHUNK, D_MODEL), F32)],
        compiler_params=_params("arbitrary"),
        name="mix",
    )(z, z, z, z, z, z, y_b, x, *weights)


def _top_rows(s, k):
    n, cols = s.shape
    row = lax.broadcasted_iota(jnp.int32, (n, cols), 0)
    rank_row = lax.broadcasted_iota(jnp.int32, (k, cols), 0)

    def body(r, carry):
        work, rank, vals = carry
        m = jnp.max(work, axis=0, keepdims=True)
        first = jnp.min(jnp.where(work == m, row, n), axis=0, keepdims=True)
        sel = row == first
        return (jnp.where(sel, -jnp.inf, work), jnp.where(sel, r, rank), jnp.where(rank_row == r, m, vals))

    init = (s, jnp.full((n, cols), k, jnp.int32), jnp.zeros((k, cols), F32))
    _, rank, vals = lax.fori_loop(0, k, body, init)
    return rank, vals


def _select_exact(s0, s1):
    k = PEER_TOPK
    rank0, vals0 = _top_rows(s0, k)
    rank1, vals1 = _top_rows(s1, k)
    cand = jnp.concatenate([vals0[a:a + 1, :] + vals1 for a in range(k)], axis=0)
    pair_rank, pair_vals = _top_rows(cand, k)
    taken = jnp.where(pair_rank < k, 1.0, 0.0)
    z = jnp.sum(taken * jnp.exp(cand - pair_vals[0:1, :]), axis=0, keepdims=True)
    n_b = [jnp.sum(taken[a * k:(a + 1) * k, :], axis=0, keepdims=True) for a in range(k)]
    nb_key = jnp.zeros(s0.shape, F32)
    for a in range(k):
        nb_key = jnp.where(rank0 == a, n_b[a], nb_key)
    e0 = jnp.where(rank0 < k, jnp.exp(s0 - vals0[0:1, :]), 0.0)
    e1 = jnp.where(rank1 < k, jnp.exp(s1 - vals1[0:1, :]), 0.0)
    return e0 / z, nb_key, e1, rank1.astype(F32)


def _larger_smaller(a, b):
    if a is None:
        return b, None
    if b is None:
        return a, None
    return jnp.maximum(a, b), jnp.minimum(a, b)


def _sort_bitonic(xs):
    xs = list(xs)
    j = len(xs) // 2
    while j >= 1:
        for i in range(len(xs)):
            if i & j == 0:
                xs[i], xs[i | j] = _larger_smaller(xs[i], xs[i | j])
        j //= 2
    return xs


def _sort_desc(xs):
    if len(xs) == 1:
        return list(xs)
    half = len(xs) // 2
    return _sort_bitonic(_sort_desc(xs[:half]) + _sort_desc(xs[half:])[::-1])


def _top_merge(xs, ys):
    n = len(xs)
    return _sort_bitonic([_larger_smaller(xs[i], ys[n - 1 - i])[0] for i in range(n)])


def _merge_keep(xs, ys, keep):
    n = 1
    while n < max(len(xs), len(ys)):
        n *= 2
    xs = list(xs) + [None] * (n - len(xs))
    ys = list(ys) + [None] * (n - len(ys))
    out = _top_merge(xs, ys) if n >= keep else _sort_bitonic(xs + ys[::-1])
    return [v for v in out if v is not None][:keep]


def _sublane_total(v):
    for shift in (1, 2, 4):
        v = v + pltpu.roll(v, shift, 0)
    return v


def _top_values(blocks):
    xs = _sort_desc(blocks)
    for shift in (1, 2, 4):
        xs = _top_merge(xs, [pltpu.roll(v, shift, 0) for v in xs])
    return xs


def _select_distinct(s0, s1):
    k = PEER_TOPK
    sub = s0.shape[0] // k
    x0 = [s0[sub * j:sub * (j + 1), :] for j in range(k)]
    x1 = [s1[sub * j:sub * (j + 1), :] for j in range(k)]
    v0, v1 = _top_values(x0), _top_values(x1)
    cand = [[v0[a] + v1[b] for b in range(k // (a + 1))] for a in range(k)]
    col = [cand[a][0] for a in range(k // 2, k)]
    rest = _merge_keep(_merge_keep(cand[1], cand[2], k),
                       _merge_keep(_merge_keep(cand[3], cand[4], k), _merge_keep(cand[5], cand[6], k), k), k)
    rest = _merge_keep(rest, _merge_keep(cand[7], col, k), k)
    top = _merge_keep(cand[0], rest, k)
    tau = top[k - 1]
    z = functools.reduce(lambda acc, v: acc + jnp.exp(v - top[0]), top[1:], jnp.ones_like(tau))
    n_b = [functools.reduce(lambda acc, c: acc + jnp.where(c >= tau, 1.0, 0.0), row, jnp.zeros_like(tau))
           for row in cand]
    count = lambda xs, v: _sublane_total(functools.reduce(
        lambda acc, x: acc + jnp.where(x >= v, 1.0, 0.0), xs, jnp.zeros_like(v)))
    distinct = jnp.where(jnp.logical_and(jnp.logical_and(count(x0, v0[k - 1]) == k, count(x1, v1[k - 1]) == k),
                                         functools.reduce(jnp.add, n_b) == k), 1.0, 0.0)
    inv_z = 1.0 / z
    gate0, nb_key, gate1, rank1 = [], [], [], []
    for j in range(k):
        nb_j = jnp.zeros_like(tau)
        r_j = jnp.full_like(tau, float(k))
        for a in range(k):
            nb_j = jnp.where(x0[j] == v0[a], n_b[a], nb_j)
            r_j = jnp.where(x1[j] == v1[a], float(a), r_j)
        nb_key.append(nb_j)
        rank1.append(r_j)
        gate0.append(jnp.exp(x0[j] - v0[0]) * inv_z)
        gate1.append(jnp.exp(x1[j] - v1[0]))
    cat = lambda xs: jnp.concatenate(xs, axis=0)
    return cat(gate0), cat(nb_key), cat(gate1), cat(rank1), distinct


def _peer_select_kernel(x_ref, g_ref, wq_ref, sk_ref, ht_ref, a_ref, nb_ref, b_ref, r1_ref, qp_ref):
    h2 = _rms(x_ref[...], g_ref[...])
    ht_ref[...] = h2.T.astype(BF16)
    qp_ref[...] = jnp.dot(h2.astype(BF16), wq_ref[...], preferred_element_type=F32).astype(BF16)

    def head(h, carry):
        s = []
        for p in range(2):
            off = pl.multiple_of((h * 2 + p) * D_HALF, D_HALF)
            s.append(lax.dot_general(sk_ref[p], qp_ref[:, pl.ds(off, D_HALF)], (((1,), (1,)), ((), ())),
                                     preferred_element_type=F32))

        def write(gate0, nb_key, gate1, rank1):
            a_ref[h] = gate0
            nb_ref[h] = nb_key
            b_ref[h] = gate1.astype(BF16)
            r1_ref[h] = rank1.astype(BF16)

        *fast, distinct = _select_distinct(s[0], s[1])
        all_distinct = jnp.min(distinct) > 0.5

        @pl.when(all_distinct)
        def _():
            write(*fast)

        @pl.when(jnp.logical_not(all_distinct))
        def _():
            write(*_select_exact(s[0], s[1]))

        return carry

    lax.fori_loop(0, PEER_HEADS, head, 0)


def _peer_select(x1, g2, wq_bf, sk_bf, tb):
    t = x1.shape[0]
    sel_spec = pl.BlockSpec((PEER_HEADS, N_KEYS, tb), lambda i: (0, 0, i))
    sel_shape = lambda dtype: jax.ShapeDtypeStruct((PEER_HEADS, N_KEYS, t), dtype)
    return pl.pallas_call(
        _peer_select_kernel,
        grid=(t // tb,),
        in_specs=[
            pl.BlockSpec((tb, D_MODEL), lambda i: (i, 0)),
            pl.BlockSpec((1, D_MODEL), lambda i: (0, 0)),
            pl.BlockSpec(wq_bf.shape, lambda i: (0, 0)),
            pl.BlockSpec(sk_bf.shape, lambda i: (0, 0, 0)),
        ],
        out_specs=[pl.BlockSpec((D_MODEL, tb), lambda i: (0, i))] + [sel_spec] * 4,
        out_shape=[jax.ShapeDtypeStruct((D_MODEL, t), BF16), sel_shape(F32), sel_shape(F32),
                   sel_shape(BF16), sel_shape(BF16)],
        scratch_shapes=[pltpu.VMEM((tb, PEER_HEADS * D_QUERY), BF16)],
        compiler_params=_params("parallel"),
        name="peer_select",
    )(x1, g2, wq_bf, sk_bf)


def _peer_dense_kernel(ht_ref, u_ref, vt_ref, a_ref, nb_ref, b_ref, r1_ref, x1_ref, gf_ref,
                       y_ref, acc_ref, gact_ref):
    c = pl.program_id(1)

    @pl.when(c == 0)
    def _():
        acc_ref[...] = jnp.zeros_like(acc_ref)

    tb = ht_ref.shape[1]
    act = _gelu_tanh(jnp.dot(u_ref[...], ht_ref[...], preferred_element_type=F32)).astype(BF16)
    tiles = N_KEYS // BF16_ROWS
    for il in range(u_ref.shape[0] // N_KEYS):
        rows = slice(il * N_KEYS, (il + 1) * N_KEYS)
        gate = None
        for h in range(PEER_HEADS):
            row = lambda ref: jnp.broadcast_to(ref[h, il:il + 1, :], (BF16_ROWS, tb)).astype(BF16)[None]
            rank1 = r1_ref[h].reshape(tiles, BF16_ROWS, tb)
            gate1 = b_ref[h].reshape(tiles, BF16_ROWS, tb)
            term = jnp.where(rank1 < row(nb_ref), row(a_ref) * gate1, jnp.zeros_like(gate1))
            gate = term if gate is None else gate + term
        gact_ref[rows, :] = gate.reshape(N_KEYS, tb) * act[rows, :]
    acc_ref[...] += jnp.dot(vt_ref[...], gact_ref[...], preferred_element_type=F32)

    @pl.when(c == pl.num_programs(1) - 1)
    def _():
        y_ref[...] = _rms(x1_ref[...] + acc_ref[...].T, gf_ref[...])


def _peer_dense(ht, u_bf, vt_bf, sel, x1, gf, tb, ec):
    t = x1.shape[0]
    a, nb, b, r1 = sel
    key_rows = ec // N_KEYS
    return pl.pallas_call(
        _peer_dense_kernel,
        grid=(t // tb, N_EXPERTS // ec),
        in_specs=[
            pl.BlockSpec((D_MODEL, tb), lambda i, c: (0, i)),
            pl.BlockSpec((ec, D_MODEL), lambda i, c: (c, 0)),
            pl.BlockSpec((D_MODEL, ec), lambda i, c: (0, c)),
            pl.BlockSpec((PEER_HEADS, key_rows, tb), lambda i, c: (0, c, i)),
            pl.BlockSpec((PEER_HEADS, key_rows, tb), lambda i, c: (0, c, i)),
            pl.BlockSpec((PEER_HEADS, N_KEYS, tb), lambda i, c: (0, 0, i)),
            pl.BlockSpec((PEER_HEADS, N_KEYS, tb), lambda i, c: (0, 0, i)),
            pl.BlockSpec((tb, D_MODEL), lambda i, c: (i, 0)),
            pl.BlockSpec((1, D_MODEL), lambda i, c: (0, 0)),
        ],
        out_specs=pl.BlockSpec((tb, D_MODEL), lambda i, c: (i, 0)),
        out_shape=jax.ShapeDtypeStruct((t, D_MODEL), F32),
        scratch_shapes=[pltpu.VMEM((D_MODEL, tb), F32), pltpu.VMEM((ec, tb), BF16)],
        compiler_params=_params("parallel", "arbitrary"),
        name="peer_dense",
    )(ht, u_bf, vt_bf, a, nb, b, r1, x1, gf)


def _token_tiles(t):
    pick = lambda pref: pref if t % pref == 0 else t
    return dict(project=pick(1024), mix=pick(256), select=pick(256), dense=pick(512))


DENSE_EXPERT_CHUNK = 2048


def _layer_tail(z, attn, x, wts, chunk_w, chunk_b_t, seq_rows):
    tiles = _token_tiles(x.shape[0])
    x1, vln = _mix(z, attn, x, wts["lnv_g"], wts["lnv_b"], chunk_w, chunk_b_t,
                   wts["w_a"], wts["w_b"], wts["w_o"], tiles["mix"], seq_rows)
    ht, *sel = _peer_select(x1, wts["norm2_g"], wts["peer_wq"], wts["peer_subkeys"], tiles["select"])
    y = _peer_dense(ht, wts["peer_u"], wts["peer_vt"], sel, x1, wts["final_norm_g"], tiles["dense"],
                    DENSE_EXPERT_CHUNK)
    return y, vln


def kernel(x_prompt, x_sample, cache_k_w128, cache_v_w128, cache_k_w512, cache_v_w512,
           cache_k_w2048, cache_v_w2048, norm1_g, w_in, lnv_g, lnv_b, w_s, b_s, w_a, w_b, w_o,
           norm2_g, peer_wq, peer_subkeys, peer_u, peer_v, final_norm_g):
    depth = w_in.shape[0]
    assert depth == 1, "one layer: the final rmsnorm is fused into the layer's last kernel"
    batch, seq, _ = x_prompt.shape
    dec_batch, dec_seq, _ = x_sample.shape
    tp, ts = batch * seq, dec_batch * dec_seq
    assert seq % (QB * DIL_RATES[-1]) == 0 and dec_seq <= min(DIL_RATES[1:]) and CHUNK % dec_seq == 0
    assert ts % CHUNK == 0 and all(c.shape[2] == w for c, w in zip((cache_k_w128, cache_k_w512, cache_k_w2048), DIL_WINDOWS))
    l = 0
    row = lambda a: a[l].reshape(1, -1)
    wts = {
        "lnv_g": row(lnv_g), "lnv_b": row(lnv_b),
        "w_a": w_a[l].astype(BF16), "w_b": w_b[l].astype(BF16), "w_o": w_o[l].astype(BF16),
        "norm2_g": row(norm2_g), "peer_wq": peer_wq[l].astype(BF16),
        "peer_subkeys": peer_subkeys[l].astype(BF16),
        "peer_u": peer_u[l].astype(BF16), "peer_vt": peer_v[l].T.astype(BF16),
        "final_norm_g": final_norm_g.reshape(1, -1),
    }
    g1 = row(norm1_g)
    w_in_bf = w_in[l].astype(BF16)

    xp = x_prompt.reshape(tp, D_MODEL)
    cos_p, sin_p = _rope_tables(jnp.arange(seq, dtype=jnp.int32))
    zp = _project(xp, g1, w_in_bf, cos_p, sin_p, _token_tiles(tp)["project"])
    attn_p = _attn_prompt(zp, batch, seq)
    yp, vln_p = _layer_tail(zp, attn_p, xp, wts, w_s[l], jnp.transpose(b_s[l]), seq)

    xs = x_sample.reshape(ts, D_MODEL)
    pos_s = PAST_LEN + jnp.tile(jnp.arange(dec_seq, dtype=jnp.int32), dec_batch)
    cos_s, sin_s = _rope_tables(pos_s)
    zs = _project(xs, g1, w_in_bf, cos_s, sin_s, _token_tiles(ts)["project"])
    cache_k = [c[l].reshape(dec_batch, w, D_B) for c, w in zip((cache_k_w128, cache_k_w512, cache_k_w2048), DIL_WINDOWS)]
    cache_v = [c[l].reshape(dec_batch, w, D_B) for c, w in zip((cache_v_w128, cache_v_w512, cache_v_w2048), DIL_WINDOWS)]
    attn_s = _attn_sample(zs, dec_batch, dec_seq, cache_k, cache_v)
    eye = jnp.eye(CHUNK // dec_seq, dtype=F32)
    w_s_blk = jax.vmap(lambda w: jnp.kron(eye, w))(w_s[l][:, :dec_seq, :dec_seq])
    b_s_blk_t = jnp.transpose(jnp.tile(b_s[l][:, :dec_seq], (1, CHUNK // dec_seq)))
    ys, vln_s = _layer_tail(zs, attn_s, xs, wts, w_s_blk, b_s_blk_t, CHUNK)

    def heads(a, b_, rows):
        return a.reshape(1, b_, rows, B_HEADS, HEAD_DIM)

    zp3 = zp.reshape(batch, seq, D_IN)
    zs3 = zs.reshape(dec_batch, dec_seq, D_IN)
    prompt_kv, sample_kv = [], []
    for g, w in enumerate(DIL_WINDOWS):
        keep = min(w, seq)
        for blk, cache in ((K_BLK, cache_k[g]), (V_BLK, cache_v[g])):
            cols = slice((blk + g) * COL_BLK, (blk + g + 1) * COL_BLK)
            prompt_kv.append(heads(zp3[:, seq - keep:, cols], batch, keep))
            rows = jnp.concatenate([cache, zs3[:, :, cols]], axis=1)
            keep_s = min(w, rows.shape[1])
            sample_kv.append(heads(rows[:, rows.shape[1] - keep_s:], dec_batch, keep_s))
    tail = seq - CHUNK * ((seq - 1) // CHUNK)
    prompt_gmlp_v = vln_p.reshape(1, batch, CHUNK, D_A)[:, :, CHUNK - tail:]
    sample_gmlp_v = vln_s.reshape(1, dec_batch, dec_seq, D_A)
    return (yp.reshape(batch, seq, D_MODEL), ys.reshape(dec_batch, dec_seq, D_MODEL),
            *prompt_kv, prompt_gmlp_v, *sample_kv, sample_gmlp_v)
```

```python
import functools

import jax
import jax.numpy as jnp
from jax import lax
from jax.experimental import pallas as pl
from jax.experimental.pallas import tpu as pltpu

F32 = jnp.float32
BF16 = jnp.bfloat16

D_MODEL = 1024
PAST_LEN = 16384
CHUNK = 128
D_A = D_MODEL
A_GROUPS = 8
A_GROUP_DIM = D_A // A_GROUPS
DIL_WINDOWS = (128, 512, 2048)
DIL_RATES = (1, 4, 16)
N_DIL = 3
N_BACK = 128
B_HEADS = 8
HEAD_DIM = 64
D_B = B_HEADS * HEAD_DIM
ROPE_THETA = 10000.0
N_KEYS = 128
N_EXPERTS = N_KEYS * N_KEYS
PEER_HEADS = 8
PEER_TOPK = 16
D_QUERY = 256
D_HALF = D_QUERY // 2
EPS = 1e-6
OFF_QKV = 2 * D_A
OFF_GATE = OFF_QKV + 3 * N_DIL * D_B
D_IN = OFF_GATE + 2 * D_MODEL

COL_BLK = 512
N_COL_BLK = D_IN // COL_BLK
Q_BLK = OFF_QKV // COL_BLK
K_BLK = Q_BLK + N_DIL
V_BLK = K_BLK + N_DIL
GA_BLK = OFF_GATE // COL_BLK
GB_BLK = GA_BLK + D_MODEL // COL_BLK

LANES = 128
BF16_ROWS = 16
QB = 128
SUPER = QB * DIL_RATES[-1]
ATTN_UNROLL = 8
NEG = float(jnp.finfo(jnp.float32).min)
VMEM_LIMIT = 56 * 1024 * 1024


def _params(*sem):
    return pltpu.CompilerParams(dimension_semantics=sem, vmem_limit_bytes=VMEM_LIMIT)


def _rms(x, g):
    return x * lax.rsqrt(jnp.mean(x * x, axis=-1, keepdims=True) + EPS) * g


def _gelu_tanh(x):
    c = -2.0 * (2.0 / jnp.pi) ** 0.5 * 1.4426950408889634
    return x / (1.0 + jnp.exp2(x * (c + (c * 0.044715) * (x * x))))


def _project_kernel(x_ref, g_ref, w_ref, cos_ref, sin_ref, z_ref, h_ref):
    j = pl.program_id(1)

    @pl.when(j == 0)
    def _():
        h_ref[...] = _rms(x_ref[...], g_ref[...]).astype(BF16)

    acc = jnp.dot(h_ref[...], w_ref[...], preferred_element_type=F32)
    is_rope = jnp.logical_and(j >= Q_BLK, j < V_BLK)

    @pl.when(is_rope)
    def _():
        lane = lax.broadcasted_iota(jnp.int32, acc.shape, 1)
        first_half = (lane % HEAD_DIM) < (HEAD_DIM // 2)
        partner = jnp.where(first_half,
                            pltpu.roll(acc, COL_BLK - HEAD_DIM // 2, 1),
                            pltpu.roll(acc, HEAD_DIM // 2, 1))
        z_ref[...] = acc * cos_ref[...] + partner * sin_ref[...]

    @pl.when(jnp.logical_not(is_rope))
    def _():
        z_ref[...] = acc


def _project(x, g, w_bf, cos, sin, tm):
    t = x.shape[0]
    n_pos_blk = cos.shape[0] // tm
    return pl.pallas_call(
        _project_kernel,
        grid=(t // tm, N_COL_BLK),
        in_specs=[
            pl.BlockSpec((tm, D_MODEL), lambda i, j: (i, 0)),
            pl.BlockSpec((1, D_MODEL), lambda i, j: (0, 0)),
            pl.BlockSpec((D_MODEL, COL_BLK), lambda i, j: (0, j)),
            pl.BlockSpec((tm, COL_BLK), lambda i, j: (i % n_pos_blk, 0)),
            pl.BlockSpec((tm, COL_BLK), lambda i, j: (i % n_pos_blk, 0)),
        ],
        out_specs=pl.BlockSpec((tm, COL_BLK), lambda i, j: (i, j)),
        out_shape=jax.ShapeDtypeStruct((t, D_IN), F32),
        scratch_shapes=[pltpu.VMEM((tm, D_MODEL), BF16)],
        compiler_params=_params("parallel", "arbitrary"),
        name="project",
    )(x, g, w_bf, cos, sin)


def _rope_tables(pos):
    half = HEAD_DIM // 2
    inv = 1.0 / (ROPE_THETA ** (jnp.arange(half, dtype=F32) * (2.0 / HEAD_DIM)))
    ang = pos.astype(F32)[:, None] * inv[None, :]
    cos, sin = jnp.cos(ang), jnp.sin(ang)
    cos_h = jnp.concatenate([cos, cos], axis=1)
    sin_h = jnp.concatenate([-sin, sin], axis=1)
    reps = COL_BLK // HEAD_DIM
    return jnp.tile(cos_h, (1, reps)), jnp.tile(sin_h, (1, reps))


def _softmax_rows(s):
    mx = jnp.max(s, axis=-1, keepdims=True)
    p = jnp.exp(s - mx)
    den = jnp.sum(p, axis=-1, keepdims=True)
    return p / den, mx + jnp.log(den)


def _combine_groups(outs, lses):
    mx = functools.reduce(jnp.maximum, lses)
    es = [jnp.exp(l - mx) for l in lses]
    den = functools.reduce(jnp.add, es)
    return functools.reduce(jnp.add, [(e / den) * o for e, o in zip(es, outs)])


def _attn_head_pair(q, kp, kc, vp, vc, valid):
    lane = lax.broadcasted_iota(jnp.int32, (QB, LANES), 1)
    low_head = lane < HEAD_DIM
    q2 = q * (HEAD_DIM ** -0.5)
    k2 = jnp.concatenate([kp, kc], axis=0).astype(BF16)
    v2 = jnp.concatenate([vp, vc], axis=0).astype(BF16)
    o_pair, lse_pair = [], []
    for head_mask in (low_head, jnp.logical_not(low_head)):
        qm = jnp.where(head_mask, q2, 0.0).astype(BF16)
        s = lax.dot_general(qm, k2, (((1,), (1,)), ((), ())), preferred_element_type=F32)
        p, lse = _softmax_rows(jnp.where(valid, s, NEG))
        o_pair.append(jnp.dot(p.astype(BF16), v2, preferred_element_type=F32))
        lse_pair.append(lse)
    return jnp.where(low_head, o_pair[0], o_pair[1]), jnp.where(low_head, lse_pair[0], lse_pair[1])


def _attn_prompt_kernel(*refs):
    n_g = N_DIL
    q_refs, kc_refs, vc_refs = refs[0:n_g], refs[n_g:2 * n_g], refs[2 * n_g:3 * n_g]
    kp_refs, vp_refs = refs[3 * n_g:4 * n_g], refs[4 * n_g:5 * n_g]
    y_ref, o_scr, lse_scr = refs[5 * n_g:]
    n = pl.program_id(1)
    row = lax.broadcasted_iota(jnp.int32, (QB, 2 * QB), 0)
    col = lax.broadcasted_iota(jnp.int32, (QB, 2 * QB), 1)
    back = QB + row - col
    in_band = jnp.logical_and(back >= 0, back <= N_BACK)
    first_col = jnp.where(n > 0, 0, QB)
    in_band_first = jnp.logical_and(in_band, col >= first_col)

    for g in range(n_g):
        r = DIL_RATES[g]
        span = QB * r

        def rows_at(start, r=r):
            return pl.ds(start, QB, stride=r) if r > 1 else pl.ds(start, QB)

        def store(g, rows, result):
            o_scr[g, rows, :], lse_scr[g, rows, :] = result

        def first_item(rho, carry, g=g, rows_at=rows_at):
            rows = rows_at(rho)
            store(g, rows, _attn_head_pair(q_refs[g][rows, :], kp_refs[g][rows, :], kc_refs[g][rows, :],
                                           vp_refs[g][rows, :], vc_refs[g][rows, :], in_band_first))
            return carry

        def inner_item(i, carry, g=g, r=r, span=span, rows_at=rows_at):
            start = (1 + i // r) * span + i % r
            rows, prev = rows_at(start), rows_at(start - span)
            store(g, rows, _attn_head_pair(q_refs[g][rows, :], kc_refs[g][prev, :], kc_refs[g][rows, :],
                                           vc_refs[g][prev, :], vc_refs[g][rows, :], in_band))
            return carry

        n_inner = (SUPER // span - 1) * r
        lax.fori_loop(0, r, first_item, 0, unroll=min(r, ATTN_UNROLL))
        if n_inner:
            lax.fori_loop(0, n_inner, inner_item, 0,
                          unroll=max(u for u in range(1, ATTN_UNROLL + 2) if n_inner % u == 0))

    y_ref[...] = _combine_groups([o_scr[g] for g in range(n_g)], [lse_scr[g] for g in range(n_g)])


def _attn_prompt(z, batch, seq):
    t = batch * seq
    n_super = seq // SUPER
    lane_blks = COL_BLK // LANES

    def cur(col_blk):
        return pl.BlockSpec((SUPER, LANES), lambda b, n, hp: (b * n_super + n, col_blk * lane_blks + hp))

    def prev(col_blk, r):
        span = QB * r
        per_super, per_seq = SUPER // span, seq // span
        return pl.BlockSpec((span, LANES), lambda b, n, hp: (b * per_seq + jnp.maximum(n * per_super - 1, 0),
                                                             col_blk * lane_blks + hp))

    groups = range(N_DIL)
    return pl.pallas_call(
        _attn_prompt_kernel,
        grid=(batch, n_super, D_B // LANES),
        in_specs=([cur(Q_BLK + g) for g in groups] + [cur(K_BLK + g) for g in groups]
                  + [cur(V_BLK + g) for g in groups]
                  + [prev(K_BLK + g, DIL_RATES[g]) for g in groups]
                  + [prev(V_BLK + g, DIL_RATES[g]) for g in groups]),
        out_specs=pl.BlockSpec((SUPER, LANES), lambda b, n, hp: (b * n_super + n, hp)),
        out_shape=jax.ShapeDtypeStruct((t, D_B), F32),
        scratch_shapes=[pltpu.VMEM((N_DIL, SUPER, LANES), F32)] * 2,
        compiler_params=_params("parallel", "parallel", "arbitrary"),
        name="attn_prompt",
    )(*([z] * (5 * N_DIL)))


def _attn_sample_kernel(dec_seq, *refs):
    q_refs, kn_refs, vn_refs = refs[0:3], refs[3:6], refs[6:9]
    kc_refs, vc_refs = refs[9:12], refs[12:15]
    y_ref = refs[15]
    outs, lses = [], []
    b = pl.program_id(0)
    rows = dec_seq * B_HEADS
    t_new = kn_refs[0].shape[0]
    row_h = lax.broadcasted_iota(jnp.int32, (B_HEADS, D_B), 0)
    lane_h = lax.broadcasted_iota(jnp.int32, (B_HEADS, D_B), 1) // HEAD_DIM
    head_mask = (row_h == lane_h).astype(F32)
    head_mask_rows = jnp.concatenate([head_mask] * dec_seq, axis=0)
    scale = HEAD_DIM ** -0.5
    for g in range(N_DIL):
        r, w = DIL_RATES[g], DIL_WINDOWS[g]
        q = q_refs[g][0] * scale
        q_bd = jnp.concatenate(
            [jnp.broadcast_to(q[s:s + 1, :], (B_HEADS, D_B)) * head_mask for s in range(dec_seq)],
            axis=0).astype(BF16)
        s_c = jnp.dot(q_bd, kc_refs[g][0].astype(BF16), preferred_element_type=F32)
        s_n = lax.dot_general(q_bd, kn_refs[g][...].astype(BF16), (((1,), (1,)), ((), ())),
                              preferred_element_type=F32)
        q_pos_c = lax.broadcasted_iota(jnp.int32, (rows, w), 0) // B_HEADS
        back_c = w + q_pos_c - lax.broadcasted_iota(jnp.int32, (rows, w), 1)
        valid_c = jnp.logical_and(jnp.bitwise_and(back_c, r - 1) == 0, back_c <= N_BACK * r)
        q_pos_n = lax.broadcasted_iota(jnp.int32, (rows, t_new), 0) // B_HEADS
        col_n = lax.broadcasted_iota(jnp.int32, (rows, t_new), 1)
        back_n = q_pos_n - (col_n - b * dec_seq)
        valid_n = jnp.logical_and(jnp.logical_and(back_n >= 0, col_n >= b * dec_seq),
                                  jnp.bitwise_and(back_n, r - 1) == 0)
        s_all = jnp.concatenate([jnp.where(valid_c, s_c, NEG), jnp.where(valid_n, s_n, NEG)], axis=1)
        p, lse = _softmax_rows(s_all)
        p = p.astype(BF16)
        o = (lax.dot_general(p[:, :w], vc_refs[g][0].astype(BF16), (((1,), (1,)), ((), ())),
                             preferred_element_type=F32)
             + jnp.dot(p[:, w:], vn_refs[g][...].astype(BF16), preferred_element_type=F32))
        outs.append(o)
        lses.append(lse)
    y = _combine_groups(outs, lses)
    y_ref[0] = jnp.sum((y * head_mask_rows).reshape(dec_seq, B_HEADS, D_B), axis=1)


def _attn_sample(z, dec_batch, dec_seq, cache_k, cache_v):
    t = dec_batch * dec_seq
    z3 = z.reshape(dec_batch, dec_seq, D_IN)

    def row_spec(col_blk):
        return pl.BlockSpec((1, dec_seq, COL_BLK), lambda b: (b, 0, col_blk))

    def all_spec(col_blk):
        return pl.BlockSpec((t, COL_BLK), lambda b: (0, col_blk))

    def cache_spec(w):
        return pl.BlockSpec((1, D_B, w), lambda b: (b, 0, 0))

    y = pl.pallas_call(
        functools.partial(_attn_sample_kernel, dec_seq),
        grid=(dec_batch,),
        in_specs=([row_spec(Q_BLK + g) for g in range(N_DIL)]
                  + [all_spec(K_BLK + g) for g in range(N_DIL)]
                  + [all_spec(V_BLK + g) for g in range(N_DIL)]
                  + [cache_spec(w) for w in DIL_WINDOWS] * 2),
        out_specs=pl.BlockSpec((1, dec_seq, D_B), lambda b: (b, 0, 0)),
        out_shape=jax.ShapeDtypeStruct((dec_batch, dec_seq, D_B), F32),
        compiler_params=_params("parallel"),
        name="attn_sample",
    )(*([z3] * N_DIL + [z] * (2 * N_DIL) + list(cache_k) + list(cache_v)))
    return y.reshape(t, D_B)


def _roll_cache_kernel(dec_seq, cache_ref, new_ref, out_ref):
    b = pl.program_id(0)
    w = cache_ref.shape[2]
    rolled = pltpu.roll(cache_ref[0], w - dec_seq, 1)
    new = pltpu.roll(new_ref[...], (LANES - dec_seq) - b * dec_seq, 1)
    lane = lax.broadcasted_iota(jnp.int32, new.shape, 1)
    out_ref[0] = rolled
    out_ref[0, :, w - LANES:] = jnp.where(lane >= LANES - dec_seq, new, rolled[:, w - LANES:])


def _roll_cache(cache, z_t, col_blk, dec_seq):
    dec_batch, _, w = cache.shape
    return pl.pallas_call(
        functools.partial(_roll_cache_kernel, dec_seq),
        grid=(dec_batch,),
        in_specs=[pl.BlockSpec((1, D_B, w), lambda b: (b, 0, 0)),
                  pl.BlockSpec((D_B, LANES), lambda b: (col_blk, 0))],
        out_specs=pl.BlockSpec((1, D_B, w), lambda b: (b, 0, 0)),
        out_shape=jax.ShapeDtypeStruct(cache.shape, F32),
        compiler_params=_params("parallel"),
        name="roll_cache",
    )(cache, z_t)


def _mix_kernel(u_ref, v_ref, ga0_ref, ga1_ref, gb0_ref, gb1_ref,
                yb_ref, x_ref,
                lng_ref, lnb_ref, ws_ref, bst_ref, wa_ref, wb_ref, wo_ref,
                x1_ref, vln_ref):
    tm = u_ref.shape[0]
    u = _gelu_tanh(u_ref[...])
    v = _gelu_tanh(v_ref[...])
    vc = v - jnp.mean(v, axis=-1, keepdims=True)
    var = jnp.mean(vc * vc, axis=-1, keepdims=True)
    vln = vc * lax.rsqrt(var + EPS) * lng_ref[...] + lnb_ref[...]
    vln_ref[...] = vln[tm - CHUNK:, :]

    row = lax.broadcasted_iota(jnp.int32, (CHUNK, CHUNK), 0)
    col = lax.broadcasted_iota(jnp.int32, (CHUNK, CHUNK), 1)
    causal = row >= col
    vln_bf = vln.astype(BF16)
    chunks = []
    for c in range(tm // CHUNK):
        groups = []
        for g in range(A_GROUPS):
            w = jnp.where(causal, ws_ref[g], 0.0).astype(BF16)
            vg = vln_bf[c * CHUNK:(c + 1) * CHUNK, g * A_GROUP_DIM:(g + 1) * A_GROUP_DIM]
            groups.append(jnp.dot(w, vg, preferred_element_type=F32) + bst_ref[:, g:g + 1])
        chunks.append(jnp.concatenate(groups, axis=1))
    y_a = u * jnp.concatenate(chunks, axis=0)

    a = jnp.dot(y_a.astype(BF16), wa_ref[...], preferred_element_type=F32)
    bproj = jnp.dot(yb_ref[...].astype(BF16), wb_ref[...], preferred_element_type=F32)
    g_a = jnp.concatenate([ga0_ref[...], ga1_ref[...]], axis=1)
    g_b = jnp.concatenate([gb0_ref[...], gb1_ref[...]], axis=1)
    merged = jax.nn.sigmoid(g_a) * a + jax.nn.sigmoid(g_b) * bproj
    x1_ref[...] = x_ref[...] + jnp.dot(merged.astype(BF16), wo_ref[...], preferred_element_type=F32)


def _mix(z, y_b, x, lnv_g, lnv_b, w_s, b_s_t, w_a, w_b, w_o, tm, seq_rows):
    t = x.shape[0]
    wide = lambda blk: pl.BlockSpec((tm, D_MODEL), lambda i: (i, blk))
    narrow = lambda blk: pl.BlockSpec((tm, COL_BLK), lambda i: (i, blk))
    full = lambda a: pl.BlockSpec(a.shape, lambda i: (0,) * a.ndim)
    weights = (lnv_g, lnv_b, w_s, b_s_t, w_a, w_b, w_o)
    return pl.pallas_call(
        _mix_kernel,
        grid=(t // tm,),
        in_specs=([wide(0), wide(1), narrow(GA_BLK), narrow(GA_BLK + 1), narrow(GB_BLK), narrow(GB_BLK + 1)]
                  + [narrow(0), wide(0)] + [full(a) for a in weights]),
        out_specs=[wide(0), pl.BlockSpec((CHUNK, D_MODEL), lambda i: (i * tm // seq_rows, 0))],
        out_shape=[jax.ShapeDtypeStruct((t, D_MODEL), F32),
                   jax.ShapeDtypeStruct((t // seq_rows * CHUNK, D_MODEL), F32)],
        compiler_params=_params("arbitrary"),
        name="mix",
    )(z, z, z, z, z, z, y_b, x, *weights)


def _top_rows(s, k):
    n, cols = s.shape
    row = lax.broadcasted_iota(jnp.int32, (n, cols), 0)
    rank_row = lax.broadcasted_iota(jnp.int32, (k, cols), 0)

    def body(r, carry):
        work, rank, vals = carry
        m = jnp.max(work, axis=0, keepdims=True)
        first = jnp.min(jnp.where(work == m, row, n), axis=0, keepdims=True)
        sel = row == first
        return (jnp.where(sel, -jnp.inf, work), jnp.where(sel, r, rank), jnp.where(rank_row == r, m, vals))

    init = (s, jnp.full((n, cols), k, jnp.int32), jnp.zeros((k, cols), F32))
    _, rank, vals = lax.fori_loop(0, k, body, init)
    return rank, vals


def _select_exact(s0, s1):
    k = PEER_TOPK
    rank0, vals0 = _top_rows(s0, k)
    rank1, vals1 = _top_rows(s1, k)
    cand = jnp.concatenate([vals0[a:a + 1, :] + vals1 for a in range(k)], axis=0)
    pair_rank, pair_vals = _top_rows(cand, k)
    taken = jnp.where(pair_rank < k, 1.0, 0.0)
    z = jnp.sum(taken * jnp.exp(cand - pair_vals[0:1, :]), axis=0, keepdims=True)
    n_b = [jnp.sum(taken[a * k:(a + 1) * k, :], axis=0, keepdims=True) for a in range(k)]
    nb_key = jnp.zeros(s0.shape, F32)
    for a in range(k):
        nb_key = jnp.where(rank0 == a, n_b[a], nb_key)
    e0 = jnp.where(rank0 < k, jnp.exp(s0 - vals0[0:1, :]), 0.0)
    e1 = jnp.where(rank1 < k, jnp.exp(s1 - vals1[0:1, :]), 0.0)
    return e0 / z, nb_key, e1, rank1.astype(F32)


def _larger_smaller(a, b):
    if a is None:
        return b, None
    if b is None:
        return a, None
    return jnp.maximum(a, b), jnp.minimum(a, b)


def _sort_bitonic(xs):
    xs = list(xs)
    j = len(xs) // 2
    while j >= 1:
        for i in range(len(xs)):
            if i & j == 0:
                xs[i], xs[i | j] = _larger_smaller(xs[i], xs[i | j])
        j //= 2
    return xs


def _sort_desc(xs):
    if len(xs) == 1:
        return list(xs)
    half = len(xs) // 2
    return _sort_bitonic(_sort_desc(xs[:half]) + _sort_desc(xs[half:])[::-1])


def _top_merge(xs, ys):
    n = len(xs)
    return _sort_bitonic([_larger_smaller(xs[i], ys[n - 1 - i])[0] for i in range(n)])


def _merge_keep(xs, ys, keep):
    n = 1
    while n < max(len(xs), len(ys)):
        n *= 2
    xs = list(xs) + [None] * (n - len(xs))
    ys = list(ys) + [None] * (n - len(ys))
    out = _top_merge(xs, ys) if n >= keep else _sort_bitonic(xs + ys[::-1])
    return [v for v in out if v is not None][:keep]


def _sublane_total(v):
    for shift in (1, 2, 4):
        v = v + pltpu.roll(v, shift, 0)
    return v


def _top_values(blocks):
    xs = _sort_desc(blocks)
    for shift in (1, 2, 4):
        xs = _top_merge(xs, [pltpu.roll(v, shift, 0) for v in xs])
    return xs


def _select_distinct(s0, s1):
    k = PEER_TOPK
    sub = s0.shape[0] // k
    x0 = [s0[sub * j:sub * (j + 1), :] for j in range(k)]
    x1 = [s1[sub * j:sub * (j + 1), :] for j in range(k)]
    v0, v1 = _top_values(x0), _top_values(x1)
    cand = [[v0[a] + v1[b] for b in range(k // (a + 1))] for a in range(k)]
    col = [cand[a][0] for a in range(k // 2, k)]
    rest = _merge_keep(_merge_keep(cand[1], cand[2], k),
                       _merge_keep(_merge_keep(cand[3], cand[4], k), _merge_keep(cand[5], cand[6], k), k), k)
    rest = _merge_keep(rest, _merge_keep(cand[7], col, k), k)
    top = _merge_keep(cand[0], rest, k)
    tau = top[k - 1]
    z = functools.reduce(lambda acc, v: acc + jnp.exp(v - top[0]), top[1:], jnp.ones_like(tau))
    n_b = [functools.reduce(lambda acc, c: acc + jnp.where(c >= tau, 1.0, 0.0), row, jnp.zeros_like(tau))
           for row in cand]
    count = lambda xs, v: _sublane_total(functools.reduce(
        lambda acc, x: acc + jnp.where(x >= v, 1.0, 0.0), xs, jnp.zeros_like(v)))
    distinct = jnp.where(jnp.logical_and(jnp.logical_and(count(x0, v0[k - 1]) == k, count(x1, v1[k - 1]) == k),
                                         functools.reduce(jnp.add, n_b) == k), 1.0, 0.0)
    inv_z = 1.0 / z
    gate0, nb_key, gate1, rank1 = [], [], [], []
    for j in range(k):
        nb_j = jnp.zeros_like(tau)
        r_j = jnp.full_like(tau, float(k))
        for a in range(k):
            nb_j = jnp.where(x0[j] == v0[a], n_b[a], nb_j)
            r_j = jnp.where(x1[j] == v1[a], float(a), r_j)
        nb_key.append(nb_j)
        rank1.append(r_j)
        gate0.append(jnp.exp(x0[j] - v0[0]) * inv_z)
        gate1.append(jnp.exp(x1[j] - v1[0]))
    cat = lambda xs: jnp.concatenate(xs, axis=0)
    return cat(gate0), cat(nb_key), cat(gate1), cat(rank1), distinct


def _peer_select_kernel(x_ref, g_ref, wq_ref, sk_ref, ht_ref, a_ref, nb_ref, b_ref, r1_ref, qp_ref):
    h2 = _rms(x_ref[...], g_ref[...])
    ht_ref[...] = h2.T.astype(BF16)
    qp_ref[...] = jnp.dot(h2.astype(BF16), wq_ref[...], preferred_element_type=F32).astype(BF16)

    def head(h, carry):
        s = []
        for p in range(2):
            off = pl.multiple_of((h * 2 + p) * D_HALF, D_HALF)
            s.append(lax.dot_general(sk_ref[p], qp_ref[:, pl.ds(off, D_HALF)], (((1,), (1,)), ((), ())),
                                     preferred_element_type=F32))

        def write(gate0, nb_key, gate1, rank1):
            a_ref[h] = gate0
            nb_ref[h] = nb_key
            b_ref[h] = gate1.astype(BF16)
            r1_ref[h] = rank1.astype(BF16)

        *fast, distinct = _select_distinct(s[0], s[1])
        all_distinct = jnp.min(distinct) > 0.5

        @pl.when(all_distinct)
        def _():
            write(*fast)

        @pl.when(jnp.logical_not(all_distinct))
        def _():
            write(*_select_exact(s[0], s[1]))

        return carry

    lax.fori_loop(0, PEER_HEADS, head, 0)


def _peer_select(x1, g2, wq_bf, sk_bf, tb):
    t = x1.shape[0]
    sel_spec = pl.BlockSpec((PEER_HEADS, N_KEYS, tb), lambda i: (0, 0, i))
    sel_shape = lambda dtype: jax.ShapeDtypeStruct((PEER_HEADS, N_KEYS, t), dtype)
    return pl.pallas_call(
        _peer_select_kernel,
        grid=(t // tb,),
        in_specs=[
            pl.BlockSpec((tb, D_MODEL), lambda i: (i, 0)),
            pl.BlockSpec((1, D_MODEL), lambda i: (0, 0)),
            pl.BlockSpec(wq_bf.shape, lambda i: (0, 0)),
            pl.BlockSpec(sk_bf.shape, lambda i: (0, 0, 0)),
        ],
        out_specs=[pl.BlockSpec((D_MODEL, tb), lambda i: (0, i))] + [sel_spec] * 4,
        out_shape=[jax.ShapeDtypeStruct((D_MODEL, t), BF16), sel_shape(F32), sel_shape(F32),
                   sel_shape(BF16), sel_shape(BF16)],
        scratch_shapes=[pltpu.VMEM((tb, PEER_HEADS * D_QUERY), BF16)],
        compiler_params=_params("parallel"),
        name="peer_select",
    )(x1, g2, wq_bf, sk_bf)


def _peer_dense_kernel(ht_ref, u_ref, vt_ref, a_ref, nb_ref, b_ref, r1_ref, x1_ref, gf_ref,
                       y_ref, acc_ref, gact_ref):
    c = pl.program_id(1)

    @pl.when(c == 0)
    def _():
        acc_ref[...] = jnp.zeros_like(acc_ref)

    tb = ht_ref.shape[1]
    act = _gelu_tanh(jnp.dot(u_ref[...], ht_ref[...], preferred_element_type=F32)).astype(BF16)
    tiles = N_KEYS // BF16_ROWS
    for il in range(u_ref.shape[0] // N_KEYS):
        rows = slice(il * N_KEYS, (il + 1) * N_KEYS)
        gate = None
        for h in range(PEER_HEADS):
            row = lambda ref: jnp.broadcast_to(ref[h, il:il + 1, :], (BF16_ROWS, tb)).astype(BF16)[None]
            rank1 = r1_ref[h].reshape(tiles, BF16_ROWS, tb)
            gate1 = b_ref[h].reshape(tiles, BF16_ROWS, tb)
            term = jnp.where(rank1 < row(nb_ref), row(a_ref) * gate1, jnp.zeros_like(gate1))
            gate = term if gate is None else gate + term
        gact_ref[rows, :] = gate.reshape(N_KEYS, tb) * act[rows, :]
    acc_ref[...] += jnp.dot(vt_ref[...], gact_ref[...], preferred_element_type=F32)

    @pl.when(c == pl.num_programs(1) - 1)
    def _():
        y_ref[...] = _rms(x1_ref[...] + acc_ref[...].T, gf_ref[...])


def _peer_dense(ht, u_bf, vt_bf, sel, x1, gf, tb, ec):
    t = x1.shape[0]
    a, nb, b, r1 = sel
    key_rows = ec // N_KEYS
    return pl.pallas_call(
        _peer_dense_kernel,
        grid=(t // tb, N_EXPERTS // ec),
        in_specs=[
            pl.BlockSpec((D_MODEL, tb), lambda i, c: (0, i)),
            pl.BlockSpec((ec, D_MODEL), lambda i, c: (c, 0)),
            pl.BlockSpec((D_MODEL, ec), lambda i, c: (0, c)),
            pl.BlockSpec((PEER_HEADS, key_rows, tb), lambda i, c: (0, c, i)),
            pl.BlockSpec((PEER_HEADS, key_rows, tb), lambda i, c: (0, c, i)),
            pl.BlockSpec((PEER_HEADS, N_KEYS, tb), lambda i, c: (0, 0, i)),
            pl.BlockSpec((PEER_HEADS, N_KEYS, tb), lambda i, c: (0, 0, i)),
            pl.BlockSpec((tb, D_MODEL), lambda i, c: (i, 0)),
            pl.BlockSpec((1, D_MODEL), lambda i, c: (0, 0)),
        ],
        out_specs=pl.BlockSpec((tb, D_MODEL), lambda i, c: (i, 0)),
        out_shape=jax.ShapeDtypeStruct((t, D_MODEL), F32),
        scratch_shapes=[pltpu.VMEM((D_MODEL, tb), F32), pltpu.VMEM((ec, tb), BF16)],
        compiler_params=_params("parallel", "arbitrary"),
        name="peer_dense",
    )(ht, u_bf, vt_bf, a, nb, b, r1, x1, gf)


def _token_tiles(t):
    pick = lambda pref: pref if t % pref == 0 else t
    return dict(project=pick(1024), mix=pick(256), select=pick(256), dense=pick(512))


DENSE_EXPERT_CHUNK = 2048


def _layer_tail(z, attn, x, wts, chunk_w, chunk_b_t, seq_rows):
    tiles = _token_tiles(x.shape[0])
    x1, vln = _mix(z, attn, x, wts["lnv_g"], wts["lnv_b"], chunk_w, chunk_b_t,
                   wts["w_a"], wts["w_b"], wts["w_o"], tiles["mix"], seq_rows)
    ht, *sel = _peer_select(x1, wts["norm2_g"], wts["peer_wq"], wts["peer_subkeys"], tiles["select"])
    y = _peer_dense(ht, wts["peer_u"], wts["peer_vt"], sel, x1, wts["final_norm_g"], tiles["dense"],
                    DENSE_EXPERT_CHUNK)
    return y, vln


def kernel(x_prompt, x_sample, cache_k_w128, cache_v_w128, cache_k_w512, cache_v_w512,
           cache_k_w2048, cache_v_w2048, norm1_g, w_in, lnv_g, lnv_b, w_s, b_s, w_a, w_b, w_o,
           norm2_g, peer_wq, peer_subkeys, peer_u, peer_v, final_norm_g):
    depth = w_in.shape[0]
    assert depth == 1, "one layer: the final rmsnorm is fused into the layer's last kernel"
    batch, seq, _ = x_prompt.shape
    dec_batch, dec_seq, _ = x_sample.shape
    tp, ts = batch * seq, dec_batch * dec_seq
    assert seq % (QB * DIL_RATES[-1]) == 0 and dec_seq <= min(DIL_RATES[1:]) and CHUNK % dec_seq == 0
    assert ts == CHUNK == LANES and all(c.shape[2] == w for c, w in zip((cache_k_w128, cache_k_w512, cache_k_w2048), DIL_WINDOWS))
    l = 0
    row = lambda a: a[l].reshape(1, -1)
    wts = {
        "lnv_g": row(lnv_g), "lnv_b": row(lnv_b),
        "w_a": w_a[l].astype(BF16), "w_b": w_b[l].astype(BF16), "w_o": w_o[l].astype(BF16),
        "norm2_g": row(norm2_g), "peer_wq": peer_wq[l].astype(BF16),
        "peer_subkeys": peer_subkeys[l].astype(BF16),
        "peer_u": peer_u[l].astype(BF16), "peer_vt": peer_v[l].T.astype(BF16),
        "final_norm_g": final_norm_g.reshape(1, -1),
    }
    g1 = row(norm1_g)
    w_in_bf = w_in[l].astype(BF16)

    xp = x_prompt.reshape(tp, D_MODEL)
    cos_p, sin_p = _rope_tables(jnp.arange(seq, dtype=jnp.int32))
    zp = _project(xp, g1, w_in_bf, cos_p, sin_p, _token_tiles(tp)["project"])
    attn_p = _attn_prompt(zp, batch, seq)
    yp, vln_p = _layer_tail(zp, attn_p, xp, wts, w_s[l], jnp.transpose(b_s[l]), seq)

    xs = x_sample.reshape(ts, D_MODEL)
    pos_s = PAST_LEN + jnp.tile(jnp.arange(dec_seq, dtype=jnp.int32), dec_batch)
    cos_s, sin_s = _rope_tables(pos_s)
    zs = _project(xs, g1, w_in_bf, cos_s, sin_s, _token_tiles(ts)["project"])
    rows_last = lambda c: jnp.transpose(c[l], (0, 2, 3, 1)).reshape(dec_batch, D_B, c.shape[2])
    cache_k = [rows_last(c) for c in (cache_k_w128, cache_k_w512, cache_k_w2048)]
    cache_v = [rows_last(c) for c in (cache_v_w128, cache_v_w512, cache_v_w2048)]
    attn_s = _attn_sample(zs, dec_batch, dec_seq, cache_k, cache_v)
    zs_t = jnp.transpose(zs)
    eye = jnp.eye(CHUNK // dec_seq, dtype=F32)
    w_s_blk = jax.vmap(lambda w: jnp.kron(eye, w))(w_s[l][:, :dec_seq, :dec_seq])
    b_s_blk_t = jnp.transpose(jnp.tile(b_s[l][:, :dec_seq], (1, CHUNK // dec_seq)))
    ys, vln_s = _layer_tail(zs, attn_s, xs, wts, w_s_blk, b_s_blk_t, CHUNK)

    def heads(a, b_, rows):
        return a.reshape(1, b_, rows, B_HEADS, HEAD_DIM)

    zp3 = zp.reshape(batch, seq, D_IN)
    prompt_kv, sample_kv = [], []
    for g, w in enumerate(DIL_WINDOWS):
        keep = min(w, seq)
        for blk, cache in ((K_BLK, cache_k[g]), (V_BLK, cache_v[g])):
            cols = slice((blk + g) * COL_BLK, (blk + g + 1) * COL_BLK)
            prompt_kv.append(heads(zp3[:, seq - keep:, cols], batch, keep))
            rolled = _roll_cache(cache, zs_t, blk + g, dec_seq).reshape(dec_batch, B_HEADS, HEAD_DIM, w)
            sample_kv.append(jnp.transpose(rolled, (0, 3, 1, 2))[None])
    tail = seq - CHUNK * ((seq - 1) // CHUNK)
    prompt_gmlp_v = vln_p.reshape(1, batch, CHUNK, D_A)[:, :, CHUNK - tail:]
    sample_gmlp_v = vln_s.reshape(1, dec_batch, dec_seq, D_A)
    return (yp.reshape(batch, seq, D_MODEL), ys.reshape(dec_batch, dec_seq, D_MODEL),
            *prompt_kv, prompt_gmlp_v, *sample_kv, sample_gmlp_v)
```

```python
import functools

import jax
import jax.numpy as jnp
from jax import lax
from jax.experimental import pallas as pl
from jax.experimental.pallas import tpu as pltpu

F32 = jnp.float32
BF16 = jnp.bfloat16

D_MODEL = 1024
PAST_LEN = 16384
CHUNK = 128
D_A = D_MODEL
A_GROUPS = 8
A_GROUP_DIM = D_A // A_GROUPS
DIL_WINDOWS = (128, 512, 2048)
DIL_RATES = (1, 4, 16)
N_DIL = 3
N_BACK = 128
B_HEADS = 8
HEAD_DIM = 64
D_B = B_HEADS * HEAD_DIM
ROPE_THETA = 10000.0
N_KEYS = 128
N_EXPERTS = N_KEYS * N_KEYS
PEER_HEADS = 8
PEER_TOPK = 16
D_QUERY = 256
D_HALF = D_QUERY // 2
EPS = 1e-6
OFF_QKV = 2 * D_A
OFF_GATE = OFF_QKV + 3 * N_DIL * D_B
D_IN = OFF_GATE + 2 * D_MODEL

COL_BLK = 512
N_COL_BLK = D_IN // COL_BLK
Q_BLK = OFF_QKV // COL_BLK
K_BLK = Q_BLK + N_DIL
V_BLK = K_BLK + N_DIL
GA_BLK = OFF_GATE // COL_BLK
GB_BLK = GA_BLK + D_MODEL // COL_BLK

LANES = 128
BF16_ROWS = 16
QB = 128
SUPER = QB * DIL_RATES[-1]
ATTN_UNROLL = 8
NEG = float(jnp.finfo(jnp.float32).min)
VMEM_LIMIT = 56 * 1024 * 1024


def _params(*sem):
    return pltpu.CompilerParams(dimension_semantics=sem, vmem_limit_bytes=VMEM_LIMIT)


def _rms(x, g):
    return x * lax.rsqrt(jnp.mean(x * x, axis=-1, keepdims=True) + EPS) * g


def _gelu_tanh(x):
    c = -2.0 * (2.0 / jnp.pi) ** 0.5 * 1.4426950408889634
    return x / (1.0 + jnp.exp2(x * (c + (c * 0.044715) * (x * x))))


def _project_kernel(x_ref, g_ref, w_ref, cos_ref, sin_ref, z_ref, h_ref):
    j = pl.program_id(1)

    @pl.when(j == 0)
    def _():
        h_ref[...] = _rms(x_ref[...], g_ref[...]).astype(BF16)

    acc = jnp.dot(h_ref[...], w_ref[...], preferred_element_type=F32)
    is_rope = jnp.logical_and(j >= Q_BLK, j < V_BLK)

    @pl.when(is_rope)
    def _():
        lane = lax.broadcasted_iota(jnp.int32, acc.shape, 1)
        first_half = (lane % HEAD_DIM) < (HEAD_DIM // 2)
        partner = jnp.where(first_half,
                            pltpu.roll(acc, COL_BLK - HEAD_DIM // 2, 1),
                            pltpu.roll(acc, HEAD_DIM // 2, 1))
        z_ref[...] = acc * cos_ref[...] + partner * sin_ref[...]

    @pl.when(jnp.logical_not(is_rope))
    def _():
        z_ref[...] = acc


def _project(x, g, w_bf, cos, sin, tm):
    t = x.shape[0]
    n_pos_blk = cos.shape[0] // tm
    return pl.pallas_call(
        _project_kernel,
        grid=(t // tm, N_COL_BLK),
        in_specs=[
            pl.BlockSpec((tm, D_MODEL), lambda i, j: (i, 0)),
            pl.BlockSpec((1, D_MODEL), lambda i, j: (0, 0)),
            pl.BlockSpec((D_MODEL, COL_BLK), lambda i, j: (0, j)),
            pl.BlockSpec((tm, COL_BLK), lambda i, j: (i % n_pos_blk, 0)),
            pl.BlockSpec((tm, COL_BLK), lambda i, j: (i % n_pos_blk, 0)),
        ],
        out_specs=pl.BlockSpec((tm, COL_BLK), lambda i, j: (i, j)),
        out_shape=jax.ShapeDtypeStruct((t, D_IN), F32),
        scratch_shapes=[pltpu.VMEM((tm, D_MODEL), BF16)],
        compiler_params=_params("parallel", "arbitrary"),
        name="project",
    )(x, g, w_bf, cos, sin)


def _rope_tables(pos):
    half = HEAD_DIM // 2
    inv = 1.0 / (ROPE_THETA ** (jnp.arange(half, dtype=F32) * (2.0 / HEAD_DIM)))
    ang = pos.astype(F32)[:, None] * inv[None, :]
    cos, sin = jnp.cos(ang), jnp.sin(ang)
    cos_h = jnp.concatenate([cos, cos], axis=1)
    sin_h = jnp.concatenate([-sin, sin], axis=1)
    reps = COL_BLK // HEAD_DIM
    return jnp.tile(cos_h, (1, reps)), jnp.tile(sin_h, (1, reps))


def _softmax_rows(s):
    mx = jnp.max(s, axis=-1, keepdims=True)
    p = jnp.exp(s - mx)
    den = jnp.sum(p, axis=-1, keepdims=True)
    return p / den, mx + jnp.log(den)


def _combine_groups(outs, lses):
    mx = functools.reduce(jnp.maximum, lses)
    es = [jnp.exp(l - mx) for l in lses]
    den = functools.reduce(jnp.add, es)
    return functools.reduce(jnp.add, [(e / den) * o for e, o in zip(es, outs)])


def _attn_head_pair(q, kp, kc, vp, vc, valid):
    lane = lax.broadcasted_iota(jnp.int32, (QB, LANES), 1)
    low_head = lane < HEAD_DIM
    q2 = q * (HEAD_DIM ** -0.5)
    k2 = jnp.concatenate([kp, kc], axis=0).astype(BF16)
    v2 = jnp.concatenate([vp, vc], axis=0).astype(BF16)
    o_pair, lse_pair = [], []
    for head_mask in (low_head, jnp.logical_not(low_head)):
        qm = jnp.where(head_mask, q2, 0.0).astype(BF16)
        s = lax.dot_general(qm, k2, (((1,), (1,)), ((), ())), preferred_element_type=F32)
        p, lse = _softmax_rows(jnp.where(valid, s, NEG))
        o_pair.append(jnp.dot(p.astype(BF16), v2, preferred_element_type=F32))
        lse_pair.append(lse)
    return jnp.where(low_head, o_pair[0], o_pair[1]), jnp.where(low_head, lse_pair[0], lse_pair[1])


def _attn_prompt_kernel(*refs):
    n_g = N_DIL
    q_refs, kc_refs, vc_refs = refs[0:n_g], refs[n_g:2 * n_g], refs[2 * n_g:3 * n_g]
    kp_refs, vp_refs = refs[3 * n_g:4 * n_g], refs[4 * n_g:5 * n_g]
    y_ref, o_scr, lse_scr = refs[5 * n_g:]
    n = pl.program_id(1)
    row = lax.broadcasted_iota(jnp.int32, (QB, 2 * QB), 0)
    col = lax.broadcasted_iota(jnp.int32, (QB, 2 * QB), 1)
    back = QB + row - col
    in_band = jnp.logical_and(back >= 0, back <= N_BACK)
    first_col = jnp.where(n > 0, 0, QB)
    in_band_first = jnp.logical_and(in_band, col >= first_col)

    for g in range(n_g):
        r = DIL_RATES[g]
        span = QB * r

        def rows_at(start, r=r):
            return pl.ds(start, QB, stride=r) if r > 1 else pl.ds(start, QB)

        def store(g, rows, result):
            o_scr[g, rows, :], lse_scr[g, rows, :] = result

        def first_item(rho, carry, g=g, rows_at=rows_at):
            rows = rows_at(rho)
            store(g, rows, _attn_head_pair(q_refs[g][rows, :], kp_refs[g][rows, :], kc_refs[g][rows, :],
                                           vp_refs[g][rows, :], vc_refs[g][rows, :], in_band_first))
            return carry

        def inner_item(i, carry, g=g, r=r, span=span, rows_at=rows_at):
            start = (1 + i // r) * span + i % r
            rows, prev = rows_at(start), rows_at(start - span)
            store(g, rows, _attn_head_pair(q_refs[g][rows, :], kc_refs[g][prev, :], kc_refs[g][rows, :],
                                           vc_refs[g][prev, :], vc_refs[g][rows, :], in_band))
            return carry

        n_inner = (SUPER // span - 1) * r
        lax.fori_loop(0, r, first_item, 0, unroll=min(r, ATTN_UNROLL))
        if n_inner:
            lax.fori_loop(0, n_inner, inner_item, 0,
                          unroll=max(u for u in range(1, ATTN_UNROLL + 2) if n_inner % u == 0))

    y_ref[...] = _combine_groups([o_scr[g] for g in range(n_g)], [lse_scr[g] for g in range(n_g)])


def _attn_prompt(z, batch, seq):
    t = batch * seq
    n_super = seq // SUPER
    lane_blks = COL_BLK // LANES

    def cur(col_blk):
        return pl.BlockSpec((SUPER, LANES), lambda b, n, hp: (b * n_super + n, col_blk * lane_blks + hp))

    def prev(col_blk, r):
        span = QB * r
        per_super, per_seq = SUPER // span, seq // span
        return pl.BlockSpec((span, LANES), lambda b, n, hp: (b * per_seq + jnp.maximum(n * per_super - 1, 0),
                                                             col_blk * lane_blks + hp))

    groups = range(N_DIL)
    return pl.pallas_call(
        _attn_prompt_kernel,
        grid=(batch, n_super, D_B // LANES),
        in_specs=([cur(Q_BLK + g) for g in groups] + [cur(K_BLK + g) for g in groups]
                  + [cur(V_BLK + g) for g in groups]
                  + [prev(K_BLK + g, DIL_RATES[g]) for g in groups]
                  + [prev(V_BLK + g, DIL_RATES[g]) for g in groups]),
        out_specs=pl.BlockSpec((SUPER, LANES), lambda b, n, hp: (b * n_super + n, hp)),
        out_shape=jax.ShapeDtypeStruct((t, D_B), F32),
        scratch_shapes=[pltpu.VMEM((N_DIL, SUPER, LANES), F32)] * 2,
        compiler_params=_params("parallel", "parallel", "arbitrary"),
        name="attn_prompt",
    )(*([z] * (5 * N_DIL)))


def _attn_sample_kernel(dec_seq, *refs):
    q_refs, kn_refs, vn_refs = refs[0:3], refs[3:6], refs[6:9]
    kc_refs, vc_refs = refs[9:12], refs[12:15]
    y_ref = refs[15]
    outs, lses = [], []
    b = pl.program_id(0)
    rows = dec_seq * B_HEADS
    t_new = kn_refs[0].shape[0]
    row_h = lax.broadcasted_iota(jnp.int32, (B_HEADS, D_B), 0)
    lane_h = lax.broadcasted_iota(jnp.int32, (B_HEADS, D_B), 1) // HEAD_DIM
    head_mask = (row_h == lane_h).astype(F32)
    head_mask_rows = jnp.concatenate([head_mask] * dec_seq, axis=0)
    scale = HEAD_DIM ** -0.5
    for g in range(N_DIL):
        r, w = DIL_RATES[g], DIL_WINDOWS[g]
        q = q_refs[g][0] * scale
        q_bd = jnp.concatenate(
            [jnp.broadcast_to(q[s:s + 1, :], (B_HEADS, D_B)) * head_mask for s in range(dec_seq)],
            axis=0).astype(BF16)
        s_c = jnp.dot(q_bd, kc_refs[g][0].astype(BF16), preferred_element_type=F32)
        s_n = lax.dot_general(q_bd, kn_refs[g][...].astype(BF16), (((1,), (1,)), ((), ())),
                              preferred_element_type=F32)
        q_pos_c = lax.broadcasted_iota(jnp.int32, (rows, w), 0) // B_HEADS
        back_c = w + q_pos_c - lax.broadcasted_iota(jnp.int32, (rows, w), 1)
        valid_c = jnp.logical_and(jnp.bitwise_and(back_c, r - 1) == 0, back_c <= N_BACK * r)
        q_pos_n = lax.broadcasted_iota(jnp.int32, (rows, t_new), 0) // B_HEADS
        col_n = lax.broadcasted_iota(jnp.int32, (rows, t_new), 1)
        back_n = q_pos_n - (col_n - b * dec_seq)
        valid_n = jnp.logical_and(jnp.logical_and(back_n >= 0, col_n >= b * dec_seq),
                                  jnp.bitwise_and(back_n, r - 1) == 0)
        s_all = jnp.concatenate([jnp.where(valid_c, s_c, NEG), jnp.where(valid_n, s_n, NEG)], axis=1)
        p, lse = _softmax_rows(s_all)
        p = p.astype(BF16)
        o = (lax.dot_general(p[:, :w], vc_refs[g][0].astype(BF16), (((1,), (1,)), ((), ())),
                             preferred_element_type=F32)
             + jnp.dot(p[:, w:], vn_refs[g][...].astype(BF16), preferred_element_type=F32))
        outs.append(o)
        lses.append(lse)
    y = _combine_groups(outs, lses)
    y_ref[0] = jnp.sum((y * head_mask_rows).reshape(dec_seq, B_HEADS, D_B), axis=1)


def _attn_sample(z, dec_batch, dec_seq, cache_k, cache_v):
    t = dec_batch * dec_seq
    z3 = z.reshape(dec_batch, dec_seq, D_IN)

    def row_spec(col_blk):
        return pl.BlockSpec((1, dec_seq, COL_BLK), lambda b: (b, 0, col_blk))

    def all_spec(col_blk):
        return pl.BlockSpec((t, COL_BLK), lambda b: (0, col_blk))

    def cache_spec(w):
        return pl.BlockSpec((1, D_B, w), lambda b: (b, 0, 0))

    y = pl.pallas_call(
        functools.partial(_attn_sample_kernel, dec_seq),
        grid=(dec_batch,),
        in_specs=([row_spec(Q_BLK + g) for g in range(N_DIL)]
                  + [all_spec(K_BLK + g) for g in range(N_DIL)]
                  + [all_spec(V_BLK + g) for g in range(N_DIL)]
                  + [cache_spec(w) for w in DIL_WINDOWS] * 2),
        out_specs=pl.BlockSpec((1, dec_seq, D_B), lambda b: (b, 0, 0)),
        out_shape=jax.ShapeDtypeStruct((dec_batch, dec_seq, D_B), F32),
        compiler_params=_params("parallel"),
        name="attn_sample",
    )(*([z3] * N_DIL + [z] * (2 * N_DIL) + list(cache_k) + list(cache_v)))
    return y.reshape(t, D_B)


def _roll_cache_kernel(dec_seq, cache_ref, new_ref, out_ref):
    per_step, _, w = cache_ref.shape
    lane = lax.broadcasted_iota(jnp.int32, new_ref.shape, 1)
    for i in range(per_step):
        b = pl.program_id(0) * per_step + i
        rolled = pltpu.roll(cache_ref[i], w - dec_seq, 1)
        new = pltpu.roll(new_ref[...], (LANES - dec_seq) - b * dec_seq, 1)
        out_ref[i] = rolled
        out_ref[i, :, w - LANES:] = jnp.where(lane >= LANES - dec_seq, new, rolled[:, w - LANES:])


def _roll_cache(cache, z_t, col_blk, dec_seq):
    dec_batch, _, w = cache.shape
    per_step = max(1, min(dec_batch, DIL_WINDOWS[-1] // w))
    return pl.pallas_call(
        functools.partial(_roll_cache_kernel, dec_seq),
        grid=(dec_batch // per_step,),
        in_specs=[pl.BlockSpec((per_step, D_B, w), lambda b: (b, 0, 0)),
                  pl.BlockSpec((D_B, LANES), lambda b: (col_blk, 0))],
        out_specs=pl.BlockSpec((per_step, D_B, w), lambda b: (b, 0, 0)),
        out_shape=jax.ShapeDtypeStruct(cache.shape, F32),
        compiler_params=_params("parallel"),
        name="roll_cache",
    )(cache, z_t)


def _mix_kernel(u_ref, v_ref, ga0_ref, ga1_ref, gb0_ref, gb1_ref,
                yb_ref, x_ref,
                lng_ref, lnb_ref, ws_ref, bst_ref, wa_ref, wb_ref, wo_ref,
                x1_ref, vln_ref):
    tm = u_ref.shape[0]
    u = _gelu_tanh(u_ref[...])
    v = _gelu_tanh(v_ref[...])
    vc = v - jnp.mean(v, axis=-1, keepdims=True)
    var = jnp.mean(vc * vc, axis=-1, keepdims=True)
    vln = vc * lax.rsqrt(var + EPS) * lng_ref[...] + lnb_ref[...]
    vln_ref[...] = vln[tm - CHUNK:, :]

    row = lax.broadcasted_iota(jnp.int32, (CHUNK, CHUNK), 0)
    col = lax.broadcasted_iota(jnp.int32, (CHUNK, CHUNK), 1)
    causal = row >= col
    vln_bf = vln.astype(BF16)
    chunks = []
    for c in range(tm // CHUNK):
        groups = []
        for g in range(A_GROUPS):
            w = jnp.where(causal, ws_ref[g], 0.0).astype(BF16)
            vg = vln_bf[c * CHUNK:(c + 1) * CHUNK, g * A_GROUP_DIM:(g + 1) * A_GROUP_DIM]
            groups.append(jnp.dot(w, vg, preferred_element_type=F32) + bst_ref[:, g:g + 1])
        chunks.append(jnp.concatenate(groups, axis=1))
    y_a = u * jnp.concatenate(chunks, axis=0)

    a = jnp.dot(y_a.astype(BF16), wa_ref[...], preferred_element_type=F32)
    bproj = jnp.dot(yb_ref[...].astype(BF16), wb_ref[...], preferred_element_type=F32)
    g_a = jnp.concatenate([ga0_ref[...], ga1_ref[...]], axis=1)
    g_b = jnp.concatenate([gb0_ref[...], gb1_ref[...]], axis=1)
    merged = jax.nn.sigmoid(g_a) * a + jax.nn.sigmoid(g_b) * bproj
    x1_ref[...] = x_ref[...] + jnp.dot(merged.astype(BF16), wo_ref[...], preferred_element_type=F32)


def _mix(z, y_b, x, lnv_g, lnv_b, w_s, b_s_t, w_a, w_b, w_o, tm, seq_rows):
    t = x.shape[0]
    wide = lambda blk: pl.BlockSpec((tm, D_MODEL), lambda i: (i, blk))
    narrow = lambda blk: pl.BlockSpec((tm, COL_BLK), lambda i: (i, blk))
    full = lambda a: pl.BlockSpec(a.shape, lambda i: (0,) * a.ndim)
    weights = (lnv_g, lnv_b, w_s, b_s_t, w_a, w_b, w_o)
    return pl.pallas_call(
        _mix_kernel,
        grid=(t // tm,),
        in_specs=([wide(0), wide(1), narrow(GA_BLK), narrow(GA_BLK + 1), narrow(GB_BLK), narrow(GB_BLK + 1)]
                  + [narrow(0), wide(0)] + [full(a) for a in weights]),
        out_specs=[wide(0), pl.BlockSpec((CHUNK, D_MODEL), lambda i: (i * tm // seq_rows, 0))],
        out_shape=[jax.ShapeDtypeStruct((t, D_MODEL), F32),
                   jax.ShapeDtypeStruct((t // seq_rows * CHUNK, D_MODEL), F32)],
        compiler_params=_params("arbitrary"),
        name="mix",
    )(z, z, z, z, z, z, y_b, x, *weights)


def _top_rows(s, k):
    n, cols = s.shape
    row = lax.broadcasted_iota(jnp.int32, (n, cols), 0)
    rank_row = lax.broadcasted_iota(jnp.int32, (k, cols), 0)

    def body(r, carry):
        work, rank, vals = carry
        m = jnp.max(work, axis=0, keepdims=True)
        first = jnp.min(jnp.where(work == m, row, n), axis=0, keepdims=True)
        sel = row == first
        return (jnp.where(sel, -jnp.inf, work), jnp.where(sel, r, rank), jnp.where(rank_row == r, m, vals))

    init = (s, jnp.full((n, cols), k, jnp.int32), jnp.zeros((k, cols), F32))
    _, rank, vals = lax.fori_loop(0, k, body, init)
    return rank, vals


def _select_exact(s0, s1):
    k = PEER_TOPK
    rank0, vals0 = _top_rows(s0, k)
    rank1, vals1 = _top_rows(s1, k)
    cand = jnp.concatenate([vals0[a:a + 1, :] + vals1 for a in range(k)], axis=0)
    pair_rank, pair_vals = _top_rows(cand, k)
    taken = jnp.where(pair_rank < k, 1.0, 0.0)
    z = jnp.sum(taken * jnp.exp(cand - pair_vals[0:1, :]), axis=0, keepdims=True)
    n_b = [jnp.sum(taken[a * k:(a + 1) * k, :], axis=0, keepdims=True) for a in range(k)]
    nb_key = jnp.zeros(s0.shape, F32)
    for a in range(k):
        nb_key = jnp.where(rank0 == a, n_b[a], nb_key)
    e0 = jnp.where(rank0 < k, jnp.exp(s0 - vals0[0:1, :]), 0.0)
    e1 = jnp.where(rank1 < k, jnp.exp(s1 - vals1[0:1, :]), 0.0)
    return e0 / z, nb_key, e1, rank1.astype(F32)


def _larger_smaller(a, b):
    if a is None:
        return b, None
    if b is None:
        return a, None
    return jnp.maximum(a, b), jnp.minimum(a, b)


def _sort_bitonic(xs):
    xs = list(xs)
    j = len(xs) // 2
    while j >= 1:
        for i in range(len(xs)):
            if i & j == 0:
                xs[i], xs[i | j] = _larger_smaller(xs[i], xs[i | j])
        j //= 2
    return xs


def _sort_desc(xs):
    if len(xs) == 1:
        return list(xs)
    half = len(xs) // 2
    return _sort_bitonic(_sort_desc(xs[:half]) + _sort_desc(xs[half:])[::-1])


def _top_merge(xs, ys):
    n = len(xs)
    return _sort_bitonic([_larger_smaller(xs[i], ys[n - 1 - i])[0] for i in range(n)])


def _merge_keep(xs, ys, keep):
    n = 1
    while n < max(len(xs), len(ys)):
        n *= 2
    xs = list(xs) + [None] * (n - len(xs))
    ys = list(ys) + [None] * (n - len(ys))
    out = _top_merge(xs, ys) if n >= keep else _sort_bitonic(xs + ys[::-1])
    return [v for v in out if v is not None][:keep]


def _sublane_total(v):
    for shift in (1, 2, 4):
        v = v + pltpu.roll(v, shift, 0)
    return v


def _top_values(blocks):
    xs = _sort_desc(blocks)
    for shift in (1, 2, 4):
        xs = _top_merge(xs, [pltpu.roll(v, shift, 0) for v in xs])
    return xs


def _select_distinct(s0, s1):
    k = PEER_TOPK
    sub = s0.shape[0] // k
    x0 = [s0[sub * j:sub * (j + 1), :] for j in range(k)]
    x1 = [s1[sub * j:sub * (j + 1), :] for j in range(k)]
    v0, v1 = _top_values(x0), _top_values(x1)
    cand = [[v0[a] + v1[b] for b in range(k // (a + 1))] for a in range(k)]
    col = [cand[a][0] for a in range(k // 2, k)]
    rest = _merge_keep(_merge_keep(cand[1], cand[2], k),
                       _merge_keep(_merge_keep(cand[3], cand[4], k), _merge_keep(cand[5], cand[6], k), k), k)
    rest = _merge_keep(rest, _merge_keep(cand[7], col, k), k)
    top = _merge_keep(cand[0], rest, k)
    tau = top[k - 1]
    z = functools.reduce(lambda acc, v: acc + jnp.exp(v - top[0]), top[1:], jnp.ones_like(tau))
    n_b = [functools.reduce(lambda acc, c: acc + jnp.where(c >= tau, 1.0, 0.0), row, jnp.zeros_like(tau))
           for row in cand]
    count = lambda xs, v: _sublane_total(functools.reduce(
        lambda acc, x: acc + jnp.where(x >= v, 1.0, 0.0), xs, jnp.zeros_like(v)))
    distinct = jnp.where(jnp.logical_and(jnp.logical_and(count(x0, v0[k - 1]) == k, count(x1, v1[k - 1]) == k),
                                         functools.reduce(jnp.add, n_b) == k), 1.0, 0.0)
    inv_z = 1.0 / z
    gate0, nb_key, gate1, rank1 = [], [], [], []
    for j in range(k):
        nb_j = jnp.zeros_like(tau)
        r_j = jnp.full_like(tau, float(k))
        for a in range(k):
            nb_j = jnp.where(x0[j] == v0[a], n_b[a], nb_j)
            r_j = jnp.where(x1[j] == v1[a], float(a), r_j)
        nb_key.append(nb_j)
        rank1.append(r_j)
        gate0.append(jnp.exp(x0[j] - v0[0]) * inv_z)
        gate1.append(jnp.exp(x1[j] - v1[0]))
    cat = lambda xs: jnp.concatenate(xs, axis=0)
    return cat(gate0), cat(nb_key), cat(gate1), cat(rank1), distinct


def _peer_select_kernel(x_ref, g_ref, wq_ref, sk_ref, ht_ref, a_ref, nb_ref, b_ref, r1_ref, qp_ref):
    h2 = _rms(x_ref[...], g_ref[...])
    ht_ref[...] = h2.T.astype(BF16)
    qp_ref[...] = jnp.dot(h2.astype(BF16), wq_ref[...], preferred_element_type=F32).astype(BF16)

    def head(h, carry):
        s = []
        for p in range(2):
            off = pl.multiple_of((h * 2 + p) * D_HALF, D_HALF)
            s.append(lax.dot_general(sk_ref[p], qp_ref[:, pl.ds(off, D_HALF)], (((1,), (1,)), ((), ())),
                                     preferred_element_type=F32))

        def write(gate0, nb_key, gate1, rank1):
            a_ref[h] = gate0
            nb_ref[h] = nb_key
            b_ref[h] = gate1.astype(BF16)
            r1_ref[h] = rank1.astype(BF16)

        *fast, distinct = _select_distinct(s[0], s[1])
        all_distinct = jnp.min(distinct) > 0.5

        @pl.when(all_distinct)
        def _():
            write(*fast)

        @pl.when(jnp.logical_not(all_distinct))
        def _():
            write(*_select_exact(s[0], s[1]))

        return carry

    lax.fori_loop(0, PEER_HEADS, head, 0)


def _peer_select(x1, g2, wq_bf, sk_bf, tb):
    t = x1.shape[0]
    sel_spec = pl.BlockSpec((PEER_HEADS, N_KEYS, tb), lambda i: (0, 0, i))
    sel_shape = lambda dtype: jax.ShapeDtypeStruct((PEER_HEADS, N_KEYS, t), dtype)
    return pl.pallas_call(
        _peer_select_kernel,
        grid=(t // tb,),
        in_specs=[
            pl.BlockSpec((tb, D_MODEL), lambda i: (i, 0)),
            pl.BlockSpec((1, D_MODEL), lambda i: (0, 0)),
            pl.BlockSpec(wq_bf.shape, lambda i: (0, 0)),
            pl.BlockSpec(sk_bf.shape, lambda i: (0, 0, 0)),
        ],
        out_specs=[pl.BlockSpec((D_MODEL, tb), lambda i: (0, i))] + [sel_spec] * 4,
        out_shape=[jax.ShapeDtypeStruct((D_MODEL, t), BF16), sel_shape(F32), sel_shape(F32),
                   sel_shape(BF16), sel_shape(BF16)],
        scratch_shapes=[pltpu.VMEM((tb, PEER_HEADS * D_QUERY), BF16)],
        compiler_params=_params("parallel"),
        name="peer_select",
    )(x1, g2, wq_bf, sk_bf)


def _peer_dense_kernel(ht_ref, u_ref, vt_ref, a_ref, nb_ref, b_ref, r1_ref, x1_ref, gf_ref,
                       y_ref, acc_ref, gact_ref, act_ref):
    c = pl.program_id(1)

    @pl.when(c == 0)
    def _():
        acc_ref[...] = jnp.zeros_like(acc_ref)

    tb = ht_ref.shape[1]
    start = 0
    for size in DENSE_ACT_PIECES:
        piece = slice(start, start + size)
        act_ref[piece, :] = _gelu_tanh(
            jnp.dot(u_ref[piece, :], ht_ref[...], preferred_element_type=F32)).astype(BF16)
        start += size
    act = act_ref
    tiles = N_KEYS // BF16_ROWS
    for il in range(u_ref.shape[0] // N_KEYS):
        rows = slice(il * N_KEYS, (il + 1) * N_KEYS)
        gate = None
        for h in range(PEER_HEADS):
            row = lambda ref: jnp.broadcast_to(ref[h, il:il + 1, :], (BF16_ROWS, tb)).astype(BF16)[None]
            rank1 = r1_ref[h].reshape(tiles, BF16_ROWS, tb)
            gate1 = b_ref[h].reshape(tiles, BF16_ROWS, tb)
            term = jnp.where(rank1 < row(nb_ref), row(a_ref) * gate1, jnp.zeros_like(gate1))
            gate = term if gate is None else gate + term
        gact_ref[rows, :] = gate.reshape(N_KEYS, tb) * act[rows, :]
    acc_ref[...] += jnp.dot(vt_ref[...], gact_ref[...], preferred_element_type=F32)

    @pl.when(c == pl.num_programs(1) - 1)
    def _():
        y_ref[...] = _rms(x1_ref[...] + acc_ref[...].T, gf_ref[...])


def _peer_dense(ht, u_bf, vt_bf, sel, x1, gf, tb, ec):
    t = x1.shape[0]
    a, nb, b, r1 = sel
    key_rows = ec // N_KEYS
    assert sum(DENSE_ACT_PIECES) == ec
    return pl.pallas_call(
        _peer_dense_kernel,
        grid=(t // tb, N_EXPERTS // ec),
        in_specs=[
            pl.BlockSpec((D_MODEL, tb), lambda i, c: (0, i)),
            pl.BlockSpec((ec, D_MODEL), lambda i, c: (c, 0)),
            pl.BlockSpec((D_MODEL, ec), lambda i, c: (0, c)),
            pl.BlockSpec((PEER_HEADS, key_rows, tb), lambda i, c: (0, c, i)),
            pl.BlockSpec((PEER_HEADS, key_rows, tb), lambda i, c: (0, c, i)),
            pl.BlockSpec((PEER_HEADS, N_KEYS, tb), lambda i, c: (0, 0, i)),
            pl.BlockSpec((PEER_HEADS, N_KEYS, tb), lambda i, c: (0, 0, i)),
            pl.BlockSpec((tb, D_MODEL), lambda i, c: (i, 0)),
            pl.BlockSpec((1, D_MODEL), lambda i, c: (0, 0)),
        ],
        out_specs=pl.BlockSpec((tb, D_MODEL), lambda i, c: (i, 0)),
        out_shape=jax.ShapeDtypeStruct((t, D_MODEL), F32),
        scratch_shapes=[pltpu.VMEM((D_MODEL, tb), F32), pltpu.VMEM((ec, tb), BF16), pltpu.VMEM((ec, tb), BF16)],
        compiler_params=_params("parallel", "arbitrary"),
        name="peer_dense",
    )(ht, u_bf, vt_bf, a, nb, b, r1, x1, gf)


def _token_tiles(t):
    pick = lambda pref: pref if t % pref == 0 else t
    return dict(project=pick(1024), mix=pick(256), select=pick(256), dense=pick(512))


DENSE_EXPERT_CHUNK = 2048
DENSE_ACT_PIECES = (256, 768, 1024)


def _layer_tail(z, attn, x, wts, chunk_w, chunk_b_t, seq_rows):
    tiles = _token_tiles(x.shape[0])
    x1, vln = _mix(z, attn, x, wts["lnv_g"], wts["lnv_b"], chunk_w, chunk_b_t,
                   wts["w_a"], wts["w_b"], wts["w_o"], tiles["mix"], seq_rows)
    ht, *sel = _peer_select(x1, wts["norm2_g"], wts["peer_wq"], wts["peer_subkeys"], tiles["select"])
    y = _peer_dense(ht, wts["peer_u"], wts["peer_vt"], sel, x1, wts["final_norm_g"], tiles["dense"],
                    DENSE_EXPERT_CHUNK)
    return y, vln


def kernel(x_prompt, x_sample, cache_k_w128, cache_v_w128, cache_k_w512, cache_v_w512,
           cache_k_w2048, cache_v_w2048, norm1_g, w_in, lnv_g, lnv_b, w_s, b_s, w_a, w_b, w_o,
           norm2_g, peer_wq, peer_subkeys, peer_u, peer_v, final_norm_g):
    depth = w_in.shape[0]
    assert depth == 1, "one layer: the final rmsnorm is fused into the layer's last kernel"
    batch, seq, _ = x_prompt.shape
    dec_batch, dec_seq, _ = x_sample.shape
    tp, ts = batch * seq, dec_batch * dec_seq
    assert seq % (QB * DIL_RATES[-1]) == 0 and dec_seq <= min(DIL_RATES[1:]) and CHUNK % dec_seq == 0
    assert ts == CHUNK == LANES and all(c.shape[2] == w for c, w in zip((cache_k_w128, cache_k_w512, cache_k_w2048), DIL_WINDOWS))
    l = 0
    row = lambda a: a[l].reshape(1, -1)
    wts = {
        "lnv_g": row(lnv_g), "lnv_b": row(lnv_b),
        "w_a": w_a[l].astype(BF16), "w_b": w_b[l].astype(BF16), "w_o": w_o[l].astype(BF16),
        "norm2_g": row(norm2_g), "peer_wq": peer_wq[l].astype(BF16),
        "peer_subkeys": peer_subkeys[l].astype(BF16),
        "peer_u": peer_u[l].astype(BF16), "peer_vt": peer_v[l].T.astype(BF16),
        "final_norm_g": final_norm_g.reshape(1, -1),
    }
    g1 = row(norm1_g)
    w_in_bf = w_in[l].astype(BF16)

    xp = x_prompt.reshape(tp, D_MODEL)
    cos_p, sin_p = _rope_tables(jnp.arange(seq, dtype=jnp.int32))
    zp = _project(xp, g1, w_in_bf, cos_p, sin_p, _token_tiles(tp)["project"])
    attn_p = _attn_prompt(zp, batch, seq)
    yp, vln_p = _layer_tail(zp, attn_p, xp, wts, w_s[l], jnp.transpose(b_s[l]), seq)

    xs = x_sample.reshape(ts, D_MODEL)
    pos_s = PAST_LEN + jnp.tile(jnp.arange(dec_seq, dtype=jnp.int32), dec_batch)
    cos_s, sin_s = _rope_tables(pos_s)
    zs = _project(xs, g1, w_in_bf, cos_s, sin_s, _token_tiles(ts)["project"])
    rows_last = lambda c: jnp.transpose(c[l], (0, 2, 3, 1)).reshape(dec_batch, D_B, c.shape[2])
    cache_k = [rows_last(c) for c in (cache_k_w128, cache_k_w512, cache_k_w2048)]
    cache_v = [rows_last(c) for c in (cache_v_w128, cache_v_w512, cache_v_w2048)]
    attn_s = _attn_sample(zs, dec_batch, dec_seq, cache_k, cache_v)
    zs_t = jnp.transpose(zs)
    eye = jnp.eye(CHUNK // dec_seq, dtype=F32)
    w_s_blk = jax.vmap(lambda w: jnp.kron(eye, w))(w_s[l][:, :dec_seq, :dec_seq])
    b_s_blk_t = jnp.transpose(jnp.tile(b_s[l][:, :dec_seq], (1, CHUNK // dec_seq)))
    ys, vln_s = _layer_tail(zs, attn_s, xs, wts, w_s_blk, b_s_blk_t, CHUNK)

    def heads(a, b_, rows):
        return a.reshape(1, b_, rows, B_HEADS, HEAD_DIM)

    zp3 = zp.reshape(batch, seq, D_IN)
    prompt_kv, sample_kv = [], []
    for g, w in enumerate(DIL_WINDOWS):
        keep = min(w, seq)
        for blk, cache in ((K_BLK, cache_k[g]), (V_BLK, cache_v[g])):
            cols = slice((blk + g) * COL_BLK, (blk + g + 1) * COL_BLK)
            prompt_kv.append(heads(zp3[:, seq - keep:, cols], batch, keep))
            rolled = _roll_cache(cache, zs_t, blk + g, dec_seq).reshape(dec_batch, B_HEADS, HEAD_DIM, w)
            sample_kv.append(jnp.transpose(rolled, (0, 3, 1, 2))[None])
    tail = seq - CHUNK * ((seq - 1) // CHUNK)
    prompt_gmlp_v = vln_p.reshape(1, batch, CHUNK, D_A)[:, :, CHUNK - tail:]
    sample_gmlp_v = vln_s.reshape(1, dec_batch, dec_seq, D_A)
    return (yp.reshape(batch, seq, D_MODEL), ys.reshape(dec_batch, dec_seq, D_MODEL),
            *prompt_kv, prompt_gmlp_v, *sample_kv, sample_gmlp_v)
```

```python
import functools

import jax
import jax.numpy as jnp
from jax import lax
from jax.experimental import pallas as pl
from jax.experimental.pallas import tpu as pltpu

F32 = jnp.float32
BF16 = jnp.bfloat16

D_MODEL = 1024
PAST_LEN = 16384
CHUNK = 128
D_A = D_MODEL
A_GROUPS = 8
A_GROUP_DIM = D_A // A_GROUPS
DIL_WINDOWS = (128, 512, 2048)
DIL_RATES = (1, 4, 16)
N_DIL = 3
N_BACK = 128
B_HEADS = 8
HEAD_DIM = 64
D_B = B_HEADS * HEAD_DIM
ROPE_THETA = 10000.0
N_KEYS = 128
N_EXPERTS = N_KEYS * N_KEYS
PEER_HEADS = 8
PEER_TOPK = 16
D_QUERY = 256
D_HALF = D_QUERY // 2
EPS = 1e-6
OFF_QKV = 2 * D_A
OFF_GATE = OFF_QKV + 3 * N_DIL * D_B
D_IN = OFF_GATE + 2 * D_MODEL

COL_BLK = 512
N_COL_BLK = D_IN // COL_BLK
Q_BLK = OFF_QKV // COL_BLK
K_BLK = Q_BLK + N_DIL
V_BLK = K_BLK + N_DIL
GA_BLK = OFF_GATE // COL_BLK
GB_BLK = GA_BLK + D_MODEL // COL_BLK

LANES = 128
BF16_ROWS = 16
ROPE_ROWS = 128
QB = 128
SUPER = QB * DIL_RATES[-1]
ATTN_UNROLL = 8
NEG = float(jnp.finfo(jnp.float32).min)
VMEM_LIMIT = 56 * 1024 * 1024


def _params(*sem):
    return pltpu.CompilerParams(dimension_semantics=sem, vmem_limit_bytes=VMEM_LIMIT)


def _rms(x, g):
    return x * lax.rsqrt(jnp.mean(x * x, axis=-1, keepdims=True) + EPS) * g


def _gelu_tanh(x):
    c = -2.0 * (2.0 / jnp.pi) ** 0.5 * 1.4426950408889634
    return x / (1.0 + jnp.exp2(x * (c + (c * 0.044715) * (x * x))))


def _project_kernel(x_ref, g_ref, w_ref, cos_ref, sin_ref, z_ref, h_ref):
    j = pl.program_id(1)

    @pl.when(j == 0)
    def _():
        h_ref[...] = _rms(x_ref[...], g_ref[...]).astype(BF16)

    is_rope = jnp.logical_and(j >= Q_BLK, j < V_BLK)

    @pl.when(is_rope)
    def _():
        for p in range(x_ref.shape[0] // ROPE_ROWS):
            rows = slice(p * ROPE_ROWS, (p + 1) * ROPE_ROWS)
            acc = jnp.dot(h_ref[rows, :], w_ref[...], preferred_element_type=F32)
            lane = lax.broadcasted_iota(jnp.int32, acc.shape, 1)
            first_half = (lane % HEAD_DIM) < (HEAD_DIM // 2)
            partner = jnp.where(first_half,
                                pltpu.roll(acc, COL_BLK - HEAD_DIM // 2, 1),
                                pltpu.roll(acc, HEAD_DIM // 2, 1))
            z_ref[rows, :] = acc * cos_ref[rows, :] + partner * sin_ref[rows, :]

    @pl.when(jnp.logical_not(is_rope))
    def _():
        z_ref[...] = jnp.dot(h_ref[...], w_ref[...], preferred_element_type=F32)


def _project(x, g, w_bf, cos, sin, tm):
    t = x.shape[0]
    n_pos_blk = cos.shape[0] // tm
    return pl.pallas_call(
        _project_kernel,
        grid=(t // tm, N_COL_BLK),
        in_specs=[
            pl.BlockSpec((tm, D_MODEL), lambda i, j: (i, 0)),
            pl.BlockSpec((1, D_MODEL), lambda i, j: (0, 0)),
            pl.BlockSpec((D_MODEL, COL_BLK), lambda i, j: (0, j)),
            pl.BlockSpec((tm, COL_BLK), lambda i, j: (i % n_pos_blk, 0)),
            pl.BlockSpec((tm, COL_BLK), lambda i, j: (i % n_pos_blk, 0)),
        ],
        out_specs=pl.BlockSpec((tm, COL_BLK), lambda i, j: (i, j)),
        out_shape=jax.ShapeDtypeStruct((t, D_IN), F32),
        scratch_shapes=[pltpu.VMEM((tm, D_MODEL), BF16)],
        compiler_params=_params("parallel", "arbitrary"),
        name="project",
    )(x, g, w_bf, cos, sin)


def _rope_tables(pos):
    half = HEAD_DIM // 2
    inv = 1.0 / (ROPE_THETA ** (jnp.arange(half, dtype=F32) * (2.0 / HEAD_DIM)))
    ang = pos.astype(F32)[:, None] * inv[None, :]
    cos, sin = jnp.cos(ang), jnp.sin(ang)
    cos_h = jnp.concatenate([cos, cos], axis=1)
    sin_h = jnp.concatenate([-sin, sin], axis=1)
    reps = COL_BLK // HEAD_DIM
    return jnp.tile(cos_h, (1, reps)), jnp.tile(sin_h, (1, reps))


def _softmax_rows(s):
    mx = jnp.max(s, axis=-1, keepdims=True)
    p = jnp.exp(s - mx)
    den = jnp.sum(p, axis=-1, keepdims=True)
    return p / den, mx + jnp.log(den)


def _combine_groups(outs, lses):
    mx = functools.reduce(jnp.maximum, lses)
    es = [jnp.exp(l - mx) for l in lses]
    den = functools.reduce(jnp.add, es)
    return functools.reduce(jnp.add, [(e / den) * o for e, o in zip(es, outs)])


def _attn_head_pair(q, kp, kc, vp, vc, valid):
    lane = lax.broadcasted_iota(jnp.int32, (QB, LANES), 1)
    low_head = lane < HEAD_DIM
    q2 = q * (HEAD_DIM ** -0.5)
    k2 = jnp.concatenate([kp, kc], axis=0).astype(BF16)
    v2 = jnp.concatenate([vp, vc], axis=0).astype(BF16)
    o_pair, lse_pair = [], []
    for head_mask in (low_head, jnp.logical_not(low_head)):
        qm = jnp.where(head_mask, q2, 0.0).astype(BF16)
        s = lax.dot_general(qm, k2, (((1,), (1,)), ((), ())), preferred_element_type=F32)
        p, lse = _softmax_rows(jnp.where(valid, s, NEG))
        o_pair.append(jnp.dot(p.astype(BF16), v2, preferred_element_type=F32))
        lse_pair.append(lse)
    return jnp.where(low_head, o_pair[0], o_pair[1]), jnp.where(low_head, lse_pair[0], lse_pair[1])


def _attn_prompt_kernel(*refs):
    n_g = N_DIL
    q_refs, kc_refs, vc_refs = refs[0:n_g], refs[n_g:2 * n_g], refs[2 * n_g:3 * n_g]
    kp_refs, vp_refs = refs[3 * n_g:4 * n_g], refs[4 * n_g:5 * n_g]
    y_ref, o_scr, lse_scr = refs[5 * n_g:]
    n = pl.program_id(1)
    row = lax.broadcasted_iota(jnp.int32, (QB, 2 * QB), 0)
    col = lax.broadcasted_iota(jnp.int32, (QB, 2 * QB), 1)
    back = QB + row - col
    in_band = jnp.logical_and(back >= 0, back <= N_BACK)
    first_col = jnp.where(n > 0, 0, QB)
    in_band_first = jnp.logical_and(in_band, col >= first_col)

    for g in range(n_g):
        r = DIL_RATES[g]
        span = QB * r

        def rows_at(start, r=r):
            return pl.ds(start, QB, stride=r) if r > 1 else pl.ds(start, QB)

        def store(g, rows, result):
            o_scr[g, rows, :], lse_scr[g, rows, :] = result

        def first_item(rho, carry, g=g, rows_at=rows_at):
            rows = rows_at(rho)
            store(g, rows, _attn_head_pair(q_refs[g][rows, :], kp_refs[g][rows, :], kc_refs[g][rows, :],
                                           vp_refs[g][rows, :], vc_refs[g][rows, :], in_band_first))
            return carry

        def inner_item(i, carry, g=g, r=r, span=span, rows_at=rows_at):
            start = (1 + i // r) * span + i % r
            rows, prev = rows_at(start), rows_at(start - span)
            store(g, rows, _attn_head_pair(q_refs[g][rows, :], kc_refs[g][prev, :], kc_refs[g][rows, :],
                                           vc_refs[g][prev, :], vc_refs[g][rows, :], in_band))
            return carry

        n_inner = (SUPER // span - 1) * r
        lax.fori_loop(0, r, first_item, 0, unroll=min(r, ATTN_UNROLL))
        if n_inner:
            lax.fori_loop(0, n_inner, inner_item, 0,
                          unroll=max(u for u in range(1, ATTN_UNROLL + 2) if n_inner % u == 0))

    y_ref[...] = _combine_groups([o_scr[g] for g in range(n_g)], [lse_scr[g] for g in range(n_g)])


def _attn_prompt(z, batch, seq):
    t = batch * seq
    n_super = seq // SUPER
    lane_blks = COL_BLK // LANES

    def cur(col_blk):
        return pl.BlockSpec((SUPER, LANES), lambda b, n, hp: (b * n_super + n, col_blk * lane_blks + hp))

    def prev(col_blk, r):
        span = QB * r
        per_super, per_seq = SUPER // span, seq // span
        return pl.BlockSpec((span, LANES), lambda b, n, hp: (b * per_seq + jnp.maximum(n * per_super - 1, 0),
                                                             col_blk * lane_blks + hp))

    groups = range(N_DIL)
    return pl.pallas_call(
        _attn_prompt_kernel,
        grid=(batch, n_super, D_B // LANES),
        in_specs=([cur(Q_BLK + g) for g in groups] + [cur(K_BLK + g) for g in groups]
                  + [cur(V_BLK + g) for g in groups]
                  + [prev(K_BLK + g, DIL_RATES[g]) for g in groups]
                  + [prev(V_BLK + g, DIL_RATES[g]) for g in groups]),
        out_specs=pl.BlockSpec((SUPER, LANES), lambda b, n, hp: (b * n_super + n, hp)),
        out_shape=jax.ShapeDtypeStruct((t, D_B), F32),
        scratch_shapes=[pltpu.VMEM((N_DIL, SUPER, LANES), F32)] * 2,
        compiler_params=_params("parallel", "parallel", "arbitrary"),
        name="attn_prompt",
    )(*([z] * (5 * N_DIL)))


def _attn_sample_kernel(dec_seq, *refs):
    q_refs, kn_refs, vn_refs = refs[0:3], refs[3:6], refs[6:9]
    kc_refs, vc_refs = refs[9:12], refs[12:15]
    y_ref = refs[15]
    outs, lses = [], []
    b = pl.program_id(0)
    rows = dec_seq * B_HEADS
    t_new = kn_refs[0].shape[0]
    row_h = lax.broadcasted_iota(jnp.int32, (B_HEADS, D_B), 0)
    lane_h = lax.broadcasted_iota(jnp.int32, (B_HEADS, D_B), 1) // HEAD_DIM
    head_mask = (row_h == lane_h).astype(F32)
    head_mask_rows = jnp.concatenate([head_mask] * dec_seq, axis=0)
    scale = HEAD_DIM ** -0.5
    for g in range(N_DIL):
        r, w = DIL_RATES[g], DIL_WINDOWS[g]
        q = q_refs[g][0] * scale
        q_bd = jnp.concatenate(
            [jnp.broadcast_to(q[s:s + 1, :], (B_HEADS, D_B)) * head_mask for s in range(dec_seq)],
            axis=0).astype(BF16)
        s_c = jnp.dot(q_bd, kc_refs[g][0].astype(BF16), preferred_element_type=F32)
        s_n = lax.dot_general(q_bd, kn_refs[g][...].astype(BF16), (((1,), (1,)), ((), ())),
                              preferred_element_type=F32)
        q_pos_c = lax.broadcasted_iota(jnp.int32, (rows, w), 0) // B_HEADS
        back_c = w + q_pos_c - lax.broadcasted_iota(jnp.int32, (rows, w), 1)
        valid_c = jnp.logical_and(jnp.bitwise_and(back_c, r - 1) == 0, back_c <= N_BACK * r)
        q_pos_n = lax.broadcasted_iota(jnp.int32, (rows, t_new), 0) // B_HEADS
        col_n = lax.broadcasted_iota(jnp.int32, (rows, t_new), 1)
        back_n = q_pos_n - (col_n - b * dec_seq)
        valid_n = jnp.logical_and(jnp.logical_and(back_n >= 0, col_n >= b * dec_seq),
                                  jnp.bitwise_and(back_n, r - 1) == 0)
        s_all = jnp.concatenate([jnp.where(valid_c, s_c, NEG), jnp.where(valid_n, s_n, NEG)], axis=1)
        p, lse = _softmax_rows(s_all)
        p = p.astype(BF16)
        o = (lax.dot_general(p[:, :w], vc_refs[g][0].astype(BF16), (((1,), (1,)), ((), ())),
                             preferred_element_type=F32)
             + jnp.dot(p[:, w:], vn_refs[g][...].astype(BF16), preferred_element_type=F32))
        outs.append(o)
        lses.append(lse)
    y = _combine_groups(outs, lses)
    y_ref[0] = jnp.sum((y * head_mask_rows).reshape(dec_seq, B_HEADS, D_B), axis=1)


def _attn_sample(z, dec_batch, dec_seq, cache_k, cache_v):
    t = dec_batch * dec_seq
    z3 = z.reshape(dec_batch, dec_seq, D_IN)

    def row_spec(col_blk):
        return pl.BlockSpec((1, dec_seq, COL_BLK), lambda b: (b, 0, col_blk))

    def all_spec(col_blk):
        return pl.BlockSpec((t, COL_BLK), lambda b: (0, col_blk))

    def cache_spec(w):
        return pl.BlockSpec((1, D_B, w), lambda b: (b, 0, 0))

    y = pl.pallas_call(
        functools.partial(_attn_sample_kernel, dec_seq),
        grid=(dec_batch,),
        in_specs=([row_spec(Q_BLK + g) for g in range(N_DIL)]
                  + [all_spec(K_BLK + g) for g in range(N_DIL)]
                  + [all_spec(V_BLK + g) for g in range(N_DIL)]
                  + [cache_spec(w) for w in DIL_WINDOWS] * 2),
        out_specs=pl.BlockSpec((1, dec_seq, D_B), lambda b: (b, 0, 0)),
        out_shape=jax.ShapeDtypeStruct((dec_batch, dec_seq, D_B), F32),
        compiler_params=_params("parallel"),
        name="attn_sample",
    )(*([z3] * N_DIL + [z] * (2 * N_DIL) + list(cache_k) + list(cache_v)))
    return y.reshape(t, D_B)


def _roll_cache_kernel(dec_seq, cache_ref, new_ref, out_ref):
    per_step, _, w = cache_ref.shape
    lane = lax.broadcasted_iota(jnp.int32, new_ref.shape, 1)
    for i in range(per_step):
        b = pl.program_id(0) * per_step + i
        rolled = pltpu.roll(cache_ref[i], w - dec_seq, 1)
        new = pltpu.roll(new_ref[...], (LANES - dec_seq) - b * dec_seq, 1)
        out_ref[i] = rolled
        out_ref[i, :, w - LANES:] = jnp.where(lane >= LANES - dec_seq, new, rolled[:, w - LANES:])


def _roll_cache(cache, z_t, col_blk, dec_seq):
    dec_batch, _, w = cache.shape
    per_step = max(1, min(dec_batch, DIL_WINDOWS[-1] // w))
    return pl.pallas_call(
        functools.partial(_roll_cache_kernel, dec_seq),
        grid=(dec_batch // per_step,),
        in_specs=[pl.BlockSpec((per_step, D_B, w), lambda b: (b, 0, 0)),
                  pl.BlockSpec((D_B, LANES), lambda b: (col_blk, 0))],
        out_specs=pl.BlockSpec((per_step, D_B, w), lambda b: (b, 0, 0)),
        out_shape=jax.ShapeDtypeStruct(cache.shape, F32),
        compiler_params=_params("parallel"),
        name="roll_cache",
    )(cache, z_t)


def _mix_kernel(u_ref, v_ref, ga0_ref, ga1_ref, gb0_ref, gb1_ref,
                yb_ref, x_ref,
                lng_ref, lnb_ref, ws_ref, bst_ref, wa_ref, wb_ref, wo_ref,
                x1_ref, vln_ref):
    tm = u_ref.shape[0]
    u = _gelu_tanh(u_ref[...])
    v = _gelu_tanh(v_ref[...])
    vc = v - jnp.mean(v, axis=-1, keepdims=True)
    var = jnp.mean(vc * vc, axis=-1, keepdims=True)
    vln = vc * lax.rsqrt(var + EPS) * lng_ref[...] + lnb_ref[...]
    vln_ref[...] = vln[tm - CHUNK:, :]

    row = lax.broadcasted_iota(jnp.int32, (CHUNK, CHUNK), 0)
    col = lax.broadcasted_iota(jnp.int32, (CHUNK, CHUNK), 1)
    causal = row >= col
    vln_bf = vln.astype(BF16)
    chunks = []
    for c in range(tm // CHUNK):
        groups = []
        for g in range(A_GROUPS):
            w = jnp.where(causal, ws_ref[g], 0.0).astype(BF16)
            vg = vln_bf[c * CHUNK:(c + 1) * CHUNK, g * A_GROUP_DIM:(g + 1) * A_GROUP_DIM]
            groups.append(jnp.dot(w, vg, preferred_element_type=F32) + bst_ref[:, g:g + 1])
        chunks.append(jnp.concatenate(groups, axis=1))
    y_a = u * jnp.concatenate(chunks, axis=0)

    a = jnp.dot(y_a.astype(BF16), wa_ref[...], preferred_element_type=F32)
    bproj = jnp.dot(yb_ref[...].astype(BF16), wb_ref[...], preferred_element_type=F32)
    g_a = jnp.concatenate([ga0_ref[...], ga1_ref[...]], axis=1)
    g_b = jnp.concatenate([gb0_ref[...], gb1_ref[...]], axis=1)
    merged = jax.nn.sigmoid(g_a) * a + jax.nn.sigmoid(g_b) * bproj
    x1_ref[...] = x_ref[...] + jnp.dot(merged.astype(BF16), wo_ref[...], preferred_element_type=F32)


def _mix(z, y_b, x, lnv_g, lnv_b, w_s, b_s_t, w_a, w_b, w_o, tm, seq_rows):
    t = x.shape[0]
    wide = lambda blk: pl.BlockSpec((tm, D_MODEL), lambda i: (i, blk))
    narrow = lambda blk: pl.BlockSpec((tm, COL_BLK), lambda i: (i, blk))
    full = lambda a: pl.BlockSpec(a.shape, lambda i: (0,) * a.ndim)
    weights = (lnv_g, lnv_b, w_s, b_s_t, w_a, w_b, w_o)
    return pl.pallas_call(
        _mix_kernel,
        grid=(t // tm,),
        in_specs=([wide(0), wide(1), narrow(GA_BLK), narrow(GA_BLK + 1), narrow(GB_BLK), narrow(GB_BLK + 1)]
                  + [narrow(0), wide(0)] + [full(a) for a in weights]),
        out_specs=[wide(0), pl.BlockSpec((CHUNK, D_MODEL), lambda i: (i * tm // seq_rows, 0))],
        out_shape=[jax.ShapeDtypeStruct((t, D_MODEL), F32),
                   jax.ShapeDtypeStruct((t // seq_rows * CHUNK, D_MODEL), F32)],
        compiler_params=_params("arbitrary"),
        name="mix",
    )(z, z, z, z, z, z, y_b, x, *weights)


def _top_rows(s, k):
    n, cols = s.shape
    row = lax.broadcasted_iota(jnp.int32, (n, cols), 0)
    rank_row = lax.broadcasted_iota(jnp.int32, (k, cols), 0)

    def body(r, carry):
        work, rank, vals = carry
        m = jnp.max(work, axis=0, keepdims=True)
        first = jnp.min(jnp.where(work == m, row, n), axis=0, keepdims=True)
        sel = row == first
        return (jnp.where(sel, -jnp.inf, work), jnp.where(sel, r, rank), jnp.where(rank_row == r, m, vals))

    init = (s, jnp.full((n, cols), k, jnp.int32), jnp.zeros((k, cols), F32))
    _, rank, vals = lax.fori_loop(0, k, body, init)
    return rank, vals


def _select_exact(s0, s1):
    k = PEER_TOPK
    rank0, vals0 = _top_rows(s0, k)
    rank1, vals1 = _top_rows(s1, k)
    cand = jnp.concatenate([vals0[a:a + 1, :] + vals1 for a in range(k)], axis=0)
    pair_rank, pair_vals = _top_rows(cand, k)
    taken = jnp.where(pair_rank < k, 1.0, 0.0)
    z = jnp.sum(taken * jnp.exp(cand - pair_vals[0:1, :]), axis=0, keepdims=True)
    n_b = [jnp.sum(taken[a * k:(a + 1) * k, :], axis=0, keepdims=True) for a in range(k)]
    nb_key = jnp.zeros(s0.shape, F32)
    for a in range(k):
        nb_key = jnp.where(rank0 == a, n_b[a], nb_key)
    e0 = jnp.where(rank0 < k, jnp.exp(s0 - vals0[0:1, :]), 0.0)
    e1 = jnp.where(rank1 < k, jnp.exp(s1 - vals1[0:1, :]), 0.0)
    return e0 / z, nb_key, e1, rank1.astype(F32)


def _larger_smaller(a, b):
    if a is None:
        return b, None
    if b is None:
        return a, None
    return jnp.maximum(a, b), jnp.minimum(a, b)


def _sort_bitonic(xs):
    xs = list(xs)
    j = len(xs) // 2
    while j >= 1:
        for i in range(len(xs)):
            if i & j == 0:
                xs[i], xs[i | j] = _larger_smaller(xs[i], xs[i | j])
        j //= 2
    return xs


def _sort_desc(xs):
    if len(xs) == 1:
        return list(xs)
    half = len(xs) // 2
    return _sort_bitonic(_sort_desc(xs[:half]) + _sort_desc(xs[half:])[::-1])


def _top_merge(xs, ys):
    n = len(xs)
    return _sort_bitonic([_larger_smaller(xs[i], ys[n - 1 - i])[0] for i in range(n)])


def _merge_keep(xs, ys, keep):
    n = 1
    while n < max(len(xs), len(ys)):
        n *= 2
    xs = list(xs) + [None] * (n - len(xs))
    ys = list(ys) + [None] * (n - len(ys))
    out = _top_merge(xs, ys) if n >= keep else _sort_bitonic(xs + ys[::-1])
    return [v for v in out if v is not None][:keep]


def _sublane_total(v):
    for shift in (1, 2, 4):
        v = v + pltpu.roll(v, shift, 0)
    return v


def _top_values(blocks):
    xs = _sort_desc(blocks)
    for shift in (1, 2, 4):
        xs = _top_merge(xs, [pltpu.roll(v, shift, 0) for v in xs])
    return xs


def _select_distinct(s0, s1):
    k = PEER_TOPK
    sub = s0.shape[0] // k
    x0 = [s0[sub * j:sub * (j + 1), :] for j in range(k)]
    x1 = [s1[sub * j:sub * (j + 1), :] for j in range(k)]
    v0, v1 = _top_values(x0), _top_values(x1)
    cand = [[v0[a] + v1[b] for b in range(k // (a + 1))] for a in range(k)]
    col = [cand[a][0] for a in range(k // 2, k)]
    rest = _merge_keep(_merge_keep(cand[1], cand[2], k),
                       _merge_keep(_merge_keep(cand[3], cand[4], k), _merge_keep(cand[5], cand[6], k), k), k)
    rest = _merge_keep(rest, _merge_keep(cand[7], col, k), k)
    top = _merge_keep(cand[0], rest, k)
    tau = top[k - 1]
    z = functools.reduce(lambda acc, v: acc + jnp.exp(v - top[0]), top[1:], jnp.ones_like(tau))
    n_b = [functools.reduce(lambda acc, c: acc + jnp.where(c >= tau, 1.0, 0.0), row, jnp.zeros_like(tau))
           for row in cand]
    count = lambda xs, v: _sublane_total(functools.reduce(
        lambda acc, x: acc + jnp.where(x >= v, 1.0, 0.0), xs, jnp.zeros_like(v)))
    distinct = jnp.where(jnp.logical_and(jnp.logical_and(count(x0, v0[k - 1]) == k, count(x1, v1[k - 1]) == k),
                                         functools.reduce(jnp.add, n_b) == k), 1.0, 0.0)
    inv_z = 1.0 / z
    gate0, nb_key, gate1, rank1 = [], [], [], []
    for j in range(k):
        nb_j = jnp.zeros_like(tau)
        r_j = jnp.full_like(tau, float(k))
        for a in range(k):
            nb_j = jnp.where(x0[j] == v0[a], n_b[a], nb_j)
            r_j = jnp.where(x1[j] == v1[a], float(a), r_j)
        nb_key.append(nb_j)
        rank1.append(r_j)
        gate0.append(jnp.exp(x0[j] - v0[0]) * inv_z)
        gate1.append(jnp.exp(x1[j] - v1[0]))
    cat = lambda xs: jnp.concatenate(xs, axis=0)
    return cat(gate0), cat(nb_key), cat(gate1), cat(rank1), distinct


def _peer_select_kernel(x_ref, g_ref, wq_ref, sk_ref, ht_ref, a_ref, nb_ref, b_ref, r1_ref, qp_ref):
    h2 = _rms(x_ref[...], g_ref[...])
    ht_ref[...] = h2.T.astype(BF16)
    qp_ref[...] = jnp.dot(h2.astype(BF16), wq_ref[...], preferred_element_type=F32).astype(BF16)

    def head(h, carry):
        s = []
        for p in range(2):
            off = pl.multiple_of((h * 2 + p) * D_HALF, D_HALF)
            s.append(lax.dot_general(sk_ref[p], qp_ref[:, pl.ds(off, D_HALF)], (((1,), (1,)), ((), ())),
                                     preferred_element_type=F32))

        def write(gate0, nb_key, gate1, rank1):
            a_ref[h] = gate0
            nb_ref[h] = nb_key
            b_ref[h] = gate1.astype(BF16)
            r1_ref[h] = rank1.astype(BF16)

        *fast, distinct = _select_distinct(s[0], s[1])
        all_distinct = jnp.min(distinct) > 0.5

        @pl.when(all_distinct)
        def _():
            write(*fast)

        @pl.when(jnp.logical_not(all_distinct))
        def _():
            write(*_select_exact(s[0], s[1]))

        return carry

    lax.fori_loop(0, PEER_HEADS, head, 0)


def _peer_select(x1, g2, wq_bf, sk_bf, tb):
    t = x1.shape[0]
    sel_spec = pl.BlockSpec((PEER_HEADS, N_KEYS, tb), lambda i: (0, 0, i))
    sel_shape = lambda dtype: jax.ShapeDtypeStruct((PEER_HEADS, N_KEYS, t), dtype)
    return pl.pallas_call(
        _peer_select_kernel,
        grid=(t // tb,),
        in_specs=[
            pl.BlockSpec((tb, D_MODEL), lambda i: (i, 0)),
            pl.BlockSpec((1, D_MODEL), lambda i: (0, 0)),
            pl.BlockSpec(wq_bf.shape, lambda i: (0, 0)),
            pl.BlockSpec(sk_bf.shape, lambda i: (0, 0, 0)),
        ],
        out_specs=[pl.BlockSpec((D_MODEL, tb), lambda i: (0, i))] + [sel_spec] * 4,
        out_shape=[jax.ShapeDtypeStruct((D_MODEL, t), BF16), sel_shape(F32), sel_shape(F32),
                   sel_shape(BF16), sel_shape(BF16)],
        scratch_shapes=[pltpu.VMEM((tb, PEER_HEADS * D_QUERY), BF16)],
        compiler_params=_params("parallel"),
        name="peer_select",
    )(x1, g2, wq_bf, sk_bf)


def _peer_dense_kernel(ht_ref, u_ref, vt_ref, a_ref, nb_ref, b_ref, r1_ref, x1_ref, gf_ref,
                       y_ref, acc_ref, gact_ref, act_ref):
    c = pl.program_id(1)

    @pl.when(c == 0)
    def _():
        acc_ref[...] = jnp.zeros_like(acc_ref)

    tb = ht_ref.shape[1]
    start = 0
    for size in DENSE_ACT_PIECES:
        piece = slice(start, start + size)
        act_ref[piece, :] = _gelu_tanh(
            jnp.dot(u_ref[piece, :], ht_ref[...], preferred_element_type=F32)).astype(BF16)
        start += size
    act = act_ref
    tiles = N_KEYS // BF16_ROWS
    for il in range(u_ref.shape[0] // N_KEYS):
        rows = slice(il * N_KEYS, (il + 1) * N_KEYS)
        gate = None
        for h in range(PEER_HEADS):
            row = lambda ref: jnp.broadcast_to(ref[h, il:il + 1, :], (BF16_ROWS, tb)).astype(BF16)[None]
            rank1 = r1_ref[h].reshape(tiles, BF16_ROWS, tb)
            gate1 = b_ref[h].reshape(tiles, BF16_ROWS, tb)
            term = jnp.where(rank1 < row(nb_ref), row(a_ref) * gate1, jnp.zeros_like(gate1))
            gate = term if gate is None else gate + term
        gact_ref[rows, :] = gate.reshape(N_KEYS, tb) * act[rows, :]
    acc_ref[...] += jnp.dot(vt_ref[...], gact_ref[...], preferred_element_type=F32)

    @pl.when(c == pl.num_programs(1) - 1)
    def _():
        y_ref[...] = _rms(x1_ref[...] + acc_ref[...].T, gf_ref[...])


def _peer_dense(ht, u_bf, vt_bf, sel, x1, gf, tb, ec):
    t = x1.shape[0]
    a, nb, b, r1 = sel
    key_rows = ec // N_KEYS
    assert sum(DENSE_ACT_PIECES) == ec
    return pl.pallas_call(
        _peer_dense_kernel,
        grid=(t // tb, N_EXPERTS // ec),
        in_specs=[
            pl.BlockSpec((D_MODEL, tb), lambda i, c: (0, i)),
            pl.BlockSpec((ec, D_MODEL), lambda i, c: (c, 0)),
            pl.BlockSpec((D_MODEL, ec), lambda i, c: (0, c)),
            pl.BlockSpec((PEER_HEADS, key_rows, tb), lambda i, c: (0, c, i)),
            pl.BlockSpec((PEER_HEADS, key_rows, tb), lambda i, c: (0, c, i)),
            pl.BlockSpec((PEER_HEADS, N_KEYS, tb), lambda i, c: (0, 0, i)),
            pl.BlockSpec((PEER_HEADS, N_KEYS, tb), lambda i, c: (0, 0, i)),
            pl.BlockSpec((tb, D_MODEL), lambda i, c: (i, 0)),
            pl.BlockSpec((1, D_MODEL), lambda i, c: (0, 0)),
        ],
        out_specs=pl.BlockSpec((tb, D_MODEL), lambda i, c: (i, 0)),
        out_shape=jax.ShapeDtypeStruct((t, D_MODEL), F32),
        scratch_shapes=[pltpu.VMEM((D_MODEL, tb), F32), pltpu.VMEM((ec, tb), BF16), pltpu.VMEM((ec, tb), BF16)],
        compiler_params=_params("parallel", "arbitrary"),
        name="peer_dense",
    )(ht, u_bf, vt_bf, a, nb, b, r1, x1, gf)


def _token_tiles(t):
    pick = lambda pref: pref if t % pref == 0 else t
    return dict(project=pick(1024), mix=pick(256), select=pick(256), dense=pick(512))


DENSE_EXPERT_CHUNK = 2048
DENSE_ACT_PIECES = (256, 768, 1024)


def _layer_tail(z, attn, x, wts, chunk_w, chunk_b_t, seq_rows):
    tiles = _token_tiles(x.shape[0])
    x1, vln = _mix(z, attn, x, wts["lnv_g"], wts["lnv_b"], chunk_w, chunk_b_t,
                   wts["w_a"], wts["w_b"], wts["w_o"], tiles["mix"], seq_rows)
    ht, *sel = _peer_select(x1, wts["norm2_g"], wts["peer_wq"], wts["peer_subkeys"], tiles["select"])
    y = _peer_dense(ht, wts["peer_u"], wts["peer_vt"], sel, x1, wts["final_norm_g"], tiles["dense"],
                    DENSE_EXPERT_CHUNK)
    return y, vln


def kernel(x_prompt, x_sample, cache_k_w128, cache_v_w128, cache_k_w512, cache_v_w512,
           cache_k_w2048, cache_v_w2048, norm1_g, w_in, lnv_g, lnv_b, w_s, b_s, w_a, w_b, w_o,
           norm2_g, peer_wq, peer_subkeys, peer_u, peer_v, final_norm_g):
    depth = w_in.shape[0]
    assert depth == 1, "one layer: the final rmsnorm is fused into the layer's last kernel"
    batch, seq, _ = x_prompt.shape
    dec_batch, dec_seq, _ = x_sample.shape
    tp, ts = batch * seq, dec_batch * dec_seq
    assert seq % (QB * DIL_RATES[-1]) == 0 and dec_seq <= min(DIL_RATES[1:]) and CHUNK % dec_seq == 0
    assert ts == CHUNK == LANES and all(c.shape[2] == w for c, w in zip((cache_k_w128, cache_k_w512, cache_k_w2048), DIL_WINDOWS))
    l = 0
    row = lambda a: a[l].reshape(1, -1)
    wts = {
        "lnv_g": row(lnv_g), "lnv_b": row(lnv_b),
        "w_a": w_a[l].astype(BF16), "w_b": w_b[l].astype(BF16), "w_o": w_o[l].astype(BF16),
        "norm2_g": row(norm2_g), "peer_wq": peer_wq[l].astype(BF16),
        "peer_subkeys": peer_subkeys[l].astype(BF16),
        "peer_u": peer_u[l].astype(BF16), "peer_vt": peer_v[l].T.astype(BF16),
        "final_norm_g": final_norm_g.reshape(1, -1),
    }
    g1 = row(norm1_g)
    w_in_bf = w_in[l].astype(BF16)

    xp = x_prompt.reshape(tp, D_MODEL)
    cos_p, sin_p = _rope_tables(jnp.arange(seq, dtype=jnp.int32))
    zp = _project(xp, g1, w_in_bf, cos_p, sin_p, _token_tiles(tp)["project"])
    attn_p = _attn_prompt(zp, batch, seq)
    yp, vln_p = _layer_tail(zp, attn_p, xp, wts, w_s[l], jnp.transpose(b_s[l]), seq)

    xs = x_sample.reshape(ts, D_MODEL)
    pos_s = PAST_LEN + jnp.tile(jnp.arange(dec_seq, dtype=jnp.int32), dec_batch)
    cos_s, sin_s = _rope_tables(pos_s)
    zs = _project(xs, g1, w_in_bf, cos_s, sin_s, _token_tiles(ts)["project"])
    rows_last = lambda c: jnp.transpose(c[l], (0, 2, 3, 1)).reshape(dec_batch, D_B, c.shape[2])
    cache_k = [rows_last(c) for c in (cache_k_w128, cache_k_w512, cache_k_w2048)]
    cache_v = [rows_last(c) for c in (cache_v_w128, cache_v_w512, cache_v_w2048)]
    attn_s = _attn_sample(zs, dec_batch, dec_seq, cache_k, cache_v)
    zs_t = jnp.transpose(zs)
    eye = jnp.eye(CHUNK // dec_seq, dtype=F32)
    w_s_blk = jax.vmap(lambda w: jnp.kron(eye, w))(w_s[l][:, :dec_seq, :dec_seq])
    b_s_blk_t = jnp.transpose(jnp.tile(b_s[l][:, :dec_seq], (1, CHUNK // dec_seq)))
    ys, vln_s = _layer_tail(zs, attn_s, xs, wts, w_s_blk, b_s_blk_t, CHUNK)

    def heads(a, b_, rows):
        return a.reshape(1, b_, rows, B_HEADS, HEAD_DIM)

    zp3 = zp.reshape(batch, seq, D_IN)
    prompt_kv, sample_kv = [], []
    for g, w in enumerate(DIL_WINDOWS):
        keep = min(w, seq)
        for blk, cache in ((K_BLK, cache_k[g]), (V_BLK, cache_v[g])):
            cols = slice((blk + g) * COL_BLK, (blk + g + 1) * COL_BLK)
            prompt_kv.append(heads(zp3[:, seq - keep:, cols], batch, keep))
            rolled = _roll_cache(cache, zs_t, blk + g, dec_seq).reshape(dec_batch, B_HEADS, HEAD_DIM, w)
            sample_kv.append(jnp.transpose(rolled, (0, 3, 1, 2))[None])
    tail = seq - CHUNK * ((seq - 1) // CHUNK)
    prompt_gmlp_v = vln_p.reshape(1, batch, CHUNK, D_A)[:, :, CHUNK - tail:]
    sample_gmlp_v = vln_s.reshape(1, dec_batch, dec_seq, D_A)
    return (yp.reshape(batch, seq, D_MODEL), ys.reshape(dec_batch, dec_seq, D_MODEL),
            *prompt_kv, prompt_gmlp_v, *sample_kv, sample_gmlp_v)
```

```python
import functools

import jax
import jax.numpy as jnp
from jax import lax
from jax.experimental import pallas as pl
from jax.experimental.pallas import tpu as pltpu

F32 = jnp.float32
BF16 = jnp.bfloat16

D_MODEL = 1024
PAST_LEN = 16384
CHUNK = 128
D_A = D_MODEL
A_GROUPS = 8
A_GROUP_DIM = D_A // A_GROUPS
DIL_WINDOWS = (128, 512, 2048)
DIL_RATES = (1, 4, 16)
N_DIL = 3
N_BACK = 128
B_HEADS = 8
HEAD_DIM = 64
D_B = B_HEADS * HEAD_DIM
ROPE_THETA = 10000.0
N_KEYS = 128
N_EXPERTS = N_KEYS * N_KEYS
PEER_HEADS = 8
PEER_TOPK = 16
D_QUERY = 256
D_HALF = D_QUERY // 2
EPS = 1e-6
OFF_QKV = 2 * D_A
OFF_GATE = OFF_QKV + 3 * N_DIL * D_B
D_IN = OFF_GATE + 2 * D_MODEL

COL_BLK = 512
N_COL_BLK = D_IN // COL_BLK
Q_BLK = OFF_QKV // COL_BLK
K_BLK = Q_BLK + N_DIL
V_BLK = K_BLK + N_DIL
GA_BLK = OFF_GATE // COL_BLK
GB_BLK = GA_BLK + D_MODEL // COL_BLK

LANES = 128
BF16_ROWS = 16
ROPE_ROWS = 128
QB = 128
SUPER = QB * DIL_RATES[-1]
ATTN_UNROLL = 8
NEG = float(jnp.finfo(jnp.float32).min)
VMEM_LIMIT = 56 * 1024 * 1024


def _params(*sem):
    return pltpu.CompilerParams(dimension_semantics=sem, vmem_limit_bytes=VMEM_LIMIT)


def _rms(x, g):
    return x * lax.rsqrt(jnp.mean(x * x, axis=-1, keepdims=True) + EPS) * g


def _gelu_tanh(x):
    c = -2.0 * (2.0 / jnp.pi) ** 0.5 * 1.4426950408889634
    return x / (1.0 + jnp.exp2(x * (c + (c * 0.044715) * (x * x))))


def _project_kernel(x_ref, g_ref, w_ref, cos_ref, sin_ref, z_ref, h_ref):
    j = pl.program_id(1)

    @pl.when(j == 0)
    def _():
        h_ref[...] = _rms(x_ref[...], g_ref[...]).astype(BF16)

    is_rope = jnp.logical_and(j >= Q_BLK, j < V_BLK)

    @pl.when(is_rope)
    def _():
        for p in range(x_ref.shape[0] // ROPE_ROWS):
            rows = slice(p * ROPE_ROWS, (p + 1) * ROPE_ROWS)
            acc = jnp.dot(h_ref[rows, :], w_ref[...], preferred_element_type=F32)
            lane = lax.broadcasted_iota(jnp.int32, acc.shape, 1)
            first_half = (lane % HEAD_DIM) < (HEAD_DIM // 2)
            partner = jnp.where(first_half,
                                pltpu.roll(acc, COL_BLK - HEAD_DIM // 2, 1),
                                pltpu.roll(acc, HEAD_DIM // 2, 1))
            z_ref[rows, :] = acc * cos_ref[rows, :] + partner * sin_ref[rows, :]

    @pl.when(jnp.logical_not(is_rope))
    def _():
        z_ref[...] = jnp.dot(h_ref[...], w_ref[...], preferred_element_type=F32)


def _project(x, g, w_bf, cos, sin, tm):
    t = x.shape[0]
    n_pos_blk = cos.shape[0] // tm
    return pl.pallas_call(
        _project_kernel,
        grid=(t // tm, N_COL_BLK),
        in_specs=[
            pl.BlockSpec((tm, D_MODEL), lambda i, j: (i, 0)),
            pl.BlockSpec((1, D_MODEL), lambda i, j: (0, 0)),
            pl.BlockSpec((D_MODEL, COL_BLK), lambda i, j: (0, j)),
            pl.BlockSpec((tm, COL_BLK), lambda i, j: (i % n_pos_blk, 0)),
            pl.BlockSpec((tm, COL_BLK), lambda i, j: (i % n_pos_blk, 0)),
        ],
        out_specs=pl.BlockSpec((tm, COL_BLK), lambda i, j: (i, j)),
        out_shape=jax.ShapeDtypeStruct((t, D_IN), F32),
        scratch_shapes=[pltpu.VMEM((tm, D_MODEL), BF16)],
        compiler_params=_params("parallel", "arbitrary"),
        name="project",
    )(x, g, w_bf, cos, sin)


def _rope_tables(pos):
    half = HEAD_DIM // 2
    inv = 1.0 / (ROPE_THETA ** (jnp.arange(half, dtype=F32) * (2.0 / HEAD_DIM)))
    ang = pos.astype(F32)[:, None] * inv[None, :]
    cos, sin = jnp.cos(ang), jnp.sin(ang)
    cos_h = jnp.concatenate([cos, cos], axis=1)
    sin_h = jnp.concatenate([-sin, sin], axis=1)
    reps = COL_BLK // HEAD_DIM
    return jnp.tile(cos_h, (1, reps)), jnp.tile(sin_h, (1, reps))


def _softmax_rows(s):
    mx = jnp.max(s, axis=-1, keepdims=True)
    p = jnp.exp(s - mx)
    den = jnp.sum(p, axis=-1, keepdims=True)
    return p / den, mx + jnp.log(den)


def _combine_groups(outs, lses):
    mx = functools.reduce(jnp.maximum, lses)
    es = [jnp.exp(l - mx) for l in lses]
    den = functools.reduce(jnp.add, es)
    return functools.reduce(jnp.add, [(e / den) * o for e, o in zip(es, outs)])


def _attn_head_pair(q, kp, kc, vp, vc, valid):
    lane = lax.broadcasted_iota(jnp.int32, (QB, LANES), 1)
    low_head = lane < HEAD_DIM
    q2 = q * (HEAD_DIM ** -0.5)
    k2 = jnp.concatenate([kp, kc], axis=0).astype(BF16)
    v2 = jnp.concatenate([vp, vc], axis=0).astype(BF16)
    o_pair, lse_pair = [], []
    for head_mask in (low_head, jnp.logical_not(low_head)):
        qm = jnp.where(head_mask, q2, 0.0).astype(BF16)
        s = lax.dot_general(qm, k2, (((1,), (1,)), ((), ())), preferred_element_type=F32)
        p, lse = _softmax_rows(jnp.where(valid, s, NEG))
        o_pair.append(jnp.dot(p.astype(BF16), v2, preferred_element_type=F32))
        lse_pair.append(lse)
    return jnp.where(low_head, o_pair[0], o_pair[1]), jnp.where(low_head, lse_pair[0], lse_pair[1])


def _attn_prompt_kernel(*refs):
    n_g = N_DIL
    q_refs, kc_refs, vc_refs = refs[0:n_g], refs[n_g:2 * n_g], refs[2 * n_g:3 * n_g]
    kp_refs, vp_refs = refs[3 * n_g:4 * n_g], refs[4 * n_g:5 * n_g]
    y_ref, o_scr, lse_scr = refs[5 * n_g:]
    n = pl.program_id(1)
    row = lax.broadcasted_iota(jnp.int32, (QB, 2 * QB), 0)
    col = lax.broadcasted_iota(jnp.int32, (QB, 2 * QB), 1)
    back = QB + row - col
    in_band = jnp.logical_and(back >= 0, back <= N_BACK)
    first_col = jnp.where(n > 0, 0, QB)
    in_band_first = jnp.logical_and(in_band, col >= first_col)

    for g in range(n_g):
        r = DIL_RATES[g]
        span = QB * r

        def rows_at(start, r=r):
            return pl.ds(start, QB, stride=r) if r > 1 else pl.ds(start, QB)

        def store(g, rows, result):
            o_scr[g, rows, :], lse_scr[g, rows, :] = result

        def first_item(rho, carry, g=g, rows_at=rows_at):
            rows = rows_at(rho)
            store(g, rows, _attn_head_pair(q_refs[g][rows, :], kp_refs[g][rows, :], kc_refs[g][rows, :],
                                           vp_refs[g][rows, :], vc_refs[g][rows, :], in_band_first))
            return carry

        def inner_item(i, carry, g=g, r=r, span=span, rows_at=rows_at):
            start = (1 + i // r) * span + i % r
            rows, prev = rows_at(start), rows_at(start - span)
            store(g, rows, _attn_head_pair(q_refs[g][rows, :], kc_refs[g][prev, :], kc_refs[g][rows, :],
                                           vc_refs[g][prev, :], vc_refs[g][rows, :], in_band))
            return carry

        n_inner = (SUPER // span - 1) * r
        lax.fori_loop(0, r, first_item, 0, unroll=min(r, ATTN_UNROLL))
        if n_inner:
            lax.fori_loop(0, n_inner, inner_item, 0,
                          unroll=max(u for u in range(1, ATTN_UNROLL + 2) if n_inner % u == 0))

    y_ref[...] = _combine_groups([o_scr[g] for g in range(n_g)], [lse_scr[g] for g in range(n_g)])


def _attn_prompt(z, batch, seq):
    t = batch * seq
    n_super = seq // SUPER
    lane_blks = COL_BLK // LANES

    def cur(col_blk):
        return pl.BlockSpec((SUPER, LANES), lambda b, n, hp: (b * n_super + n, col_blk * lane_blks + hp))

    def prev(col_blk, r):
        span = QB * r
        per_super, per_seq = SUPER // span, seq // span
        return pl.BlockSpec((span, LANES), lambda b, n, hp: (b * per_seq + jnp.maximum(n * per_super - 1, 0),
                                                             col_blk * lane_blks + hp))

    groups = range(N_DIL)
    return pl.pallas_call(
        _attn_prompt_kernel,
        grid=(batch, n_super, D_B // LANES),
        in_specs=([cur(Q_BLK + g) for g in groups] + [cur(K_BLK + g) for g in groups]
                  + [cur(V_BLK + g) for g in groups]
                  + [prev(K_BLK + g, DIL_RATES[g]) for g in groups]
                  + [prev(V_BLK + g, DIL_RATES[g]) for g in groups]),
        out_specs=pl.BlockSpec((SUPER, LANES), lambda b, n, hp: (b * n_super + n, hp)),
        out_shape=jax.ShapeDtypeStruct((t, D_B), F32),
        scratch_shapes=[pltpu.VMEM((N_DIL, SUPER, LANES), F32)] * 2,
        compiler_params=_params("parallel", "parallel", "arbitrary"),
        name="attn_prompt",
    )(*([z] * (5 * N_DIL)))


def _attn_sample_kernel(dec_seq, *refs):
    q_refs, kn_refs, vn_refs = refs[0:3], refs[3:6], refs[6:9]
    kc_refs, vc_refs = refs[9:12], refs[12:15]
    y_ref = refs[15]
    outs, lses = [], []
    b = pl.program_id(0)
    rows = dec_seq * B_HEADS
    t_new = kn_refs[0].shape[0]
    row_h = lax.broadcasted_iota(jnp.int32, (B_HEADS, D_B), 0)
    lane_h = lax.broadcasted_iota(jnp.int32, (B_HEADS, D_B), 1) // HEAD_DIM
    head_mask = (row_h == lane_h).astype(F32)
    head_mask_rows = jnp.concatenate([head_mask] * dec_seq, axis=0)
    scale = HEAD_DIM ** -0.5
    for g in range(N_DIL):
        r, w = DIL_RATES[g], DIL_WINDOWS[g]
        q = q_refs[g][0] * scale
        q_bd = jnp.concatenate(
            [jnp.broadcast_to(q[s:s + 1, :], (B_HEADS, D_B)) * head_mask for s in range(dec_seq)],
            axis=0).astype(BF16)
        s_c = jnp.dot(q_bd, kc_refs[g][0].astype(BF16), preferred_element_type=F32)
        s_n = lax.dot_general(q_bd, kn_refs[g][...].astype(BF16), (((1,), (1,)), ((), ())),
                              preferred_element_type=F32)
        q_pos_c = lax.broadcasted_iota(jnp.int32, (rows, w), 0) // B_HEADS
        back_c = w + q_pos_c - lax.broadcasted_iota(jnp.int32, (rows, w), 1)
        valid_c = jnp.logical_and(jnp.bitwise_and(back_c, r - 1) == 0, back_c <= N_BACK * r)
        q_pos_n = lax.broadcasted_iota(jnp.int32, (rows, t_new), 0) // B_HEADS
        col_n = lax.broadcasted_iota(jnp.int32, (rows, t_new), 1)
        back_n = q_pos_n - (col_n - b * dec_seq)
        valid_n = jnp.logical_and(jnp.logical_and(back_n >= 0, col_n >= b * dec_seq),
                                  jnp.bitwise_and(back_n, r - 1) == 0)
        s_all = jnp.concatenate([jnp.where(valid_c, s_c, NEG), jnp.where(valid_n, s_n, NEG)], axis=1)
        p, lse = _softmax_rows(s_all)
        p = p.astype(BF16)
        o = (lax.dot_general(p[:, :w], vc_refs[g][0].astype(BF16), (((1,), (1,)), ((), ())),
                             preferred_element_type=F32)
             + jnp.dot(p[:, w:], vn_refs[g][...].astype(BF16), preferred_element_type=F32))
        outs.append(o)
        lses.append(lse)
    y = _combine_groups(outs, lses)
    y_ref[0] = jnp.sum((y * head_mask_rows).reshape(dec_seq, B_HEADS, D_B), axis=1)


def _attn_sample(z, dec_batch, dec_seq, cache_k, cache_v):
    t = dec_batch * dec_seq
    z3 = z.reshape(dec_batch, dec_seq, D_IN)

    def row_spec(col_blk):
        return pl.BlockSpec((1, dec_seq, COL_BLK), lambda b: (b, 0, col_blk))

    def all_spec(col_blk):
        return pl.BlockSpec((t, COL_BLK), lambda b: (0, col_blk))

    def cache_spec(w):
        return pl.BlockSpec((1, D_B, w), lambda b: (b, 0, 0))

    y = pl.pallas_call(
        functools.partial(_attn_sample_kernel, dec_seq),
        grid=(dec_batch,),
        in_specs=([row_spec(Q_BLK + g) for g in range(N_DIL)]
                  + [all_spec(K_BLK + g) for g in range(N_DIL)]
                  + [all_spec(V_BLK + g) for g in range(N_DIL)]
                  + [cache_spec(w) for w in DIL_WINDOWS] * 2),
        out_specs=pl.BlockSpec((1, dec_seq, D_B), lambda b: (b, 0, 0)),
        out_shape=jax.ShapeDtypeStruct((dec_batch, dec_seq, D_B), F32),
        compiler_params=_params("parallel"),
        name="attn_sample",
    )(*([z3] * N_DIL + [z] * (2 * N_DIL) + list(cache_k) + list(cache_v)))
    return y.reshape(t, D_B)


def _roll_cache_kernel(dec_seq, cache_ref, new_ref, out_ref):
    per_step, _, w = cache_ref.shape
    lane = lax.broadcasted_iota(jnp.int32, new_ref.shape, 1)
    for i in range(per_step):
        b = pl.program_id(0) * per_step + i
        rolled = pltpu.roll(cache_ref[i], w - dec_seq, 1)
        new = pltpu.roll(new_ref[...], (LANES - dec_seq) - b * dec_seq, 1)
        out_ref[i] = rolled
        out_ref[i, :, w - LANES:] = jnp.where(lane >= LANES - dec_seq, new, rolled[:, w - LANES:])


def _roll_cache(cache, z_t, col_blk, dec_seq):
    dec_batch, _, w = cache.shape
    per_step = max(1, min(dec_batch, DIL_WINDOWS[-1] // w))
    return pl.pallas_call(
        functools.partial(_roll_cache_kernel, dec_seq),
        grid=(dec_batch // per_step,),
        in_specs=[pl.BlockSpec((per_step, D_B, w), lambda b: (b, 0, 0)),
                  pl.BlockSpec((D_B, LANES), lambda b: (col_blk, 0))],
        out_specs=pl.BlockSpec((per_step, D_B, w), lambda b: (b, 0, 0)),
        out_shape=jax.ShapeDtypeStruct(cache.shape, F32),
        compiler_params=_params("parallel"),
        name="roll_cache",
    )(cache, z_t)


def _mix_kernel(u_ref, v_ref, ga0_ref, ga1_ref, gb0_ref, gb1_ref,
                yb_ref, x_ref,
                lng_ref, lnb_ref, ws_ref, bst_ref, wa_ref, wb_ref, wo_ref,
                x1_ref, vln_ref):
    tm = u_ref.shape[0]
    u = _gelu_tanh(u_ref[...])
    v = _gelu_tanh(v_ref[...])
    vc = v - jnp.mean(v, axis=-1, keepdims=True)
    var = jnp.mean(vc * vc, axis=-1, keepdims=True)
    vln = vc * lax.rsqrt(var + EPS) * lng_ref[...] + lnb_ref[...]
    vln_ref[...] = vln[tm - CHUNK:, :]

    row = lax.broadcasted_iota(jnp.int32, (CHUNK, CHUNK), 0)
    col = lax.broadcasted_iota(jnp.int32, (CHUNK, CHUNK), 1)
    causal = row >= col
    vln_bf = vln.astype(BF16)
    chunks = []
    for c in range(tm // CHUNK):
        groups = []
        for g in range(A_GROUPS):
            w = jnp.where(causal, ws_ref[g], 0.0).astype(BF16)
            vg = vln_bf[c * CHUNK:(c + 1) * CHUNK, g * A_GROUP_DIM:(g + 1) * A_GROUP_DIM]
            groups.append(jnp.dot(w, vg, preferred_element_type=F32) + bst_ref[:, g:g + 1])
        chunks.append(jnp.concatenate(groups, axis=1))
    y_a = u * jnp.concatenate(chunks, axis=0)

    a = jnp.dot(y_a.astype(BF16), wa_ref[...], preferred_element_type=F32)
    bproj = jnp.dot(yb_ref[...].astype(BF16), wb_ref[...], preferred_element_type=F32)
    g_a = jnp.concatenate([ga0_ref[...], ga1_ref[...]], axis=1)
    g_b = jnp.concatenate([gb0_ref[...], gb1_ref[...]], axis=1)
    merged = jax.nn.sigmoid(g_a) * a + jax.nn.sigmoid(g_b) * bproj
    x1_ref[...] = x_ref[...] + jnp.dot(merged.astype(BF16), wo_ref[...], preferred_element_type=F32)


def _mix(z, y_b, x, lnv_g, lnv_b, w_s, b_s_t, w_a, w_b, w_o, tm, seq_rows):
    t = x.shape[0]
    wide = lambda blk: pl.BlockSpec((tm, D_MODEL), lambda i: (i, blk))
    narrow = lambda blk: pl.BlockSpec((tm, COL_BLK), lambda i: (i, blk))
    full = lambda a: pl.BlockSpec(a.shape, lambda i: (0,) * a.ndim)
    weights = (lnv_g, lnv_b, w_s, b_s_t, w_a, w_b, w_o)
    return pl.pallas_call(
        _mix_kernel,
        grid=(t // tm,),
        in_specs=([wide(0), wide(1), narrow(GA_BLK), narrow(GA_BLK + 1), narrow(GB_BLK), narrow(GB_BLK + 1)]
                  + [narrow(0), wide(0)] + [full(a) for a in weights]),
        out_specs=[wide(0), pl.BlockSpec((CHUNK, D_MODEL), lambda i: (i * tm // seq_rows, 0))],
        out_shape=[jax.ShapeDtypeStruct((t, D_MODEL), F32),
                   jax.ShapeDtypeStruct((t // seq_rows * CHUNK, D_MODEL), F32)],
        compiler_params=_params("arbitrary"),
        name="mix",
    )(z, z, z, z, z, z, y_b, x, *weights)


def _top_rows(s, k):
    n, cols = s.shape
    row = lax.broadcasted_iota(jnp.int32, (n, cols), 0)
    rank_row = lax.broadcasted_iota(jnp.int32, (k, cols), 0)

    def body(r, carry):
        work, rank, vals = carry
        m = jnp.max(work, axis=0, keepdims=True)
        first = jnp.min(jnp.where(work == m, row, n), axis=0, keepdims=True)
        sel = row == first
        return (jnp.where(sel, -jnp.inf, work), jnp.where(sel, r, rank), jnp.where(rank_row == r, m, vals))

    init = (s, jnp.full((n, cols), k, jnp.int32), jnp.zeros((k, cols), F32))
    _, rank, vals = lax.fori_loop(0, k, body, init)
    return rank, vals


def _select_exact(s0, s1):
    k = PEER_TOPK
    rank0, vals0 = _top_rows(s0, k)
    rank1, vals1 = _top_rows(s1, k)
    cand = jnp.concatenate([vals0[a:a + 1, :] + vals1 for a in range(k)], axis=0)
    pair_rank, pair_vals = _top_rows(cand, k)
    taken = jnp.where(pair_rank < k, 1.0, 0.0)
    z = jnp.sum(taken * jnp.exp(cand - pair_vals[0:1, :]), axis=0, keepdims=True)
    n_b = [jnp.sum(taken[a * k:(a + 1) * k, :], axis=0, keepdims=True) for a in range(k)]
    nb_key = jnp.zeros(s0.shape, F32)
    for a in range(k):
        nb_key = jnp.where(rank0 == a, n_b[a], nb_key)
    e0 = jnp.where(rank0 < k, jnp.exp(s0 - vals0[0:1, :]), 0.0)
    e1 = jnp.where(rank1 < k, jnp.exp(s1 - vals1[0:1, :]), 0.0)
    return e0 / z, nb_key, e1, rank1.astype(F32)


def _larger_smaller(a, b):
    if a is None:
        return b, None
    if b is None:
        return a, None
    return jnp.maximum(a, b), jnp.minimum(a, b)


def _sort_bitonic(xs):
    xs = list(xs)
    j = len(xs) // 2
    while j >= 1:
        for i in range(len(xs)):
            if i & j == 0:
                xs[i], xs[i | j] = _larger_smaller(xs[i], xs[i | j])
        j //= 2
    return xs


def _sort_desc(xs):
    if len(xs) == 1:
        return list(xs)
    half = len(xs) // 2
    return _sort_bitonic(_sort_desc(xs[:half]) + _sort_desc(xs[half:])[::-1])


def _top_merge(xs, ys):
    n = len(xs)
    return _sort_bitonic([_larger_smaller(xs[i], ys[n - 1 - i])[0] for i in range(n)])


def _merge_keep(xs, ys, keep):
    n = 1
    while n < max(len(xs), len(ys)):
        n *= 2
    xs = list(xs) + [None] * (n - len(xs))
    ys = list(ys) + [None] * (n - len(ys))
    out = _top_merge(xs, ys) if n >= keep else _sort_bitonic(xs + ys[::-1])
    return [v for v in out if v is not None][:keep]


def _sublane_total(v):
    for shift in (1, 2, 4):
        v = v + pltpu.roll(v, shift, 0)
    return v


def _top_values(blocks):
    xs = _sort_desc(blocks)
    for shift in (1, 2, 4):
        xs = _top_merge(xs, [pltpu.roll(v, shift, 0) for v in xs])
    return xs


def _select_distinct(s0, s1):
    k = PEER_TOPK
    sub = s0.shape[0] // k
    x0 = [s0[sub * j:sub * (j + 1), :] for j in range(k)]
    x1 = [s1[sub * j:sub * (j + 1), :] for j in range(k)]
    v0, v1 = _top_values(x0), _top_values(x1)
    cand = [[v0[a] + v1[b] for b in range(k // (a + 1))] for a in range(k)]
    col = [cand[a][0] for a in range(k // 2, k)]
    rest = _merge_keep(_merge_keep(cand[1], cand[2], k),
                       _merge_keep(_merge_keep(cand[3], cand[4], k), _merge_keep(cand[5], cand[6], k), k), k)
    rest = _merge_keep(rest, _merge_keep(cand[7], col, k), k)
    top = _merge_keep(cand[0], rest, k)
    tau = top[k - 1]
    z = functools.reduce(lambda acc, v: acc + jnp.exp(v - top[0]), top[1:], jnp.ones_like(tau))
    n_b = [functools.reduce(lambda acc, c: acc + jnp.where(c >= tau, 1.0, 0.0), row, jnp.zeros_like(tau))
           for row in cand]
    count = lambda xs, v: _sublane_total(functools.reduce(
        lambda acc, x: acc + jnp.where(x >= v, 1.0, 0.0), xs, jnp.zeros_like(v)))
    distinct = jnp.where(jnp.logical_and(jnp.logical_and(count(x0, v0[k - 1]) == k, count(x1, v1[k - 1]) == k),
                                         functools.reduce(jnp.add, n_b) == k), 1.0, 0.0)
    inv_z = 1.0 / z
    gate0, nb_key, gate1, rank1 = [], [], [], []
    for j in range(k):
        nb_j = jnp.zeros_like(tau)
        r_j = jnp.full_like(tau, float(k))
        for a in range(k):
            nb_j = jnp.where(x0[j] == v0[a], n_b[a], nb_j)
            r_j = jnp.where(x1[j] == v1[a], float(a), r_j)
        nb_key.append(nb_j)
        rank1.append(r_j)
        gate0.append(jnp.exp(x0[j] - v0[0]) * inv_z)
        gate1.append(jnp.exp(x1[j] - v1[0]))
    cat = lambda xs: jnp.concatenate(xs, axis=0)
    return cat(gate0), cat(nb_key), cat(gate1), cat(rank1), distinct


def _peer_select_kernel(x_ref, g_ref, wq_ref, sk_ref, ht_ref, a_ref, nb_ref, b_ref, r1_ref, qp_ref):
    h2 = _rms(x_ref[...], g_ref[...])
    ht_ref[...] = h2.T.astype(BF16)
    qp_ref[...] = jnp.dot(h2.astype(BF16), wq_ref[...], preferred_element_type=F32).astype(BF16)

    def head(h, carry):
        s = []
        for p in range(2):
            off = pl.multiple_of((h * 2 + p) * D_HALF, D_HALF)
            s.append(lax.dot_general(sk_ref[p], qp_ref[:, pl.ds(off, D_HALF)], (((1,), (1,)), ((), ())),
                                     preferred_element_type=F32))

        def write(gate0, nb_key, gate1, rank1):
            a_ref[h] = gate0
            nb_ref[h] = nb_key
            b_ref[h] = gate1.astype(BF16)
            r1_ref[h] = rank1.astype(BF16)

        *fast, distinct = _select_distinct(s[0], s[1])
        all_distinct = jnp.min(distinct) > 0.5

        @pl.when(all_distinct)
        def _():
            write(*fast)

        @pl.when(jnp.logical_not(all_distinct))
        def _():
            write(*_select_exact(s[0], s[1]))

        return carry

    lax.fori_loop(0, PEER_HEADS, head, 0)


def _peer_select(x1, g2, wq_bf, sk_bf, tb):
    t = x1.shape[0]
    sel_spec = pl.BlockSpec((PEER_HEADS, N_KEYS, tb), lambda i: (0, 0, i))
    sel_shape = lambda dtype: jax.ShapeDtypeStruct((PEER_HEADS, N_KEYS, t), dtype)
    return pl.pallas_call(
        _peer_select_kernel,
        grid=(t // tb,),
        in_specs=[
            pl.BlockSpec((tb, D_MODEL), lambda i: (i, 0)),
            pl.BlockSpec((1, D_MODEL), lambda i: (0, 0)),
            pl.BlockSpec(wq_bf.shape, lambda i: (0, 0)),
            pl.BlockSpec(sk_bf.shape, lambda i: (0, 0, 0)),
        ],
        out_specs=[pl.BlockSpec((D_MODEL, tb), lambda i: (0, i))] + [sel_spec] * 4,
        out_shape=[jax.ShapeDtypeStruct((D_MODEL, t), BF16), sel_shape(F32), sel_shape(F32),
                   sel_shape(BF16), sel_shape(BF16)],
        scratch_shapes=[pltpu.VMEM((tb, PEER_HEADS * D_QUERY), BF16)],
        compiler_params=_params("parallel"),
        name="peer_select",
    )(x1, g2, wq_bf, sk_bf)


def _peer_dense_kernel(ht_ref, u_ref, vt_ref, a_ref, nb_ref, b_ref, r1_ref, x1_ref, gf_ref,
                       y_ref, acc_ref, gact_ref, act_ref):
    c = pl.program_id(1)

    @pl.when(c == 0)
    def _():
        acc_ref[...] = jnp.zeros_like(acc_ref)

    tb = ht_ref.shape[1]
    start = 0
    for size in DENSE_ACT_PIECES:
        piece = slice(start, start + size)
        act_ref[piece, :] = _gelu_tanh(
            jnp.dot(u_ref[piece, :], ht_ref[...], preferred_element_type=F32)).astype(BF16)
        start += size
    act = act_ref
    tiles = N_KEYS // BF16_ROWS
    for il in range(u_ref.shape[0] // N_KEYS):
        rows = slice(il * N_KEYS, (il + 1) * N_KEYS)
        gate = None
        for h in range(PEER_HEADS):
            row = lambda ref: jnp.broadcast_to(ref[h, il:il + 1, :], (BF16_ROWS, tb)).astype(BF16)[None]
            rank1 = r1_ref[h].reshape(tiles, BF16_ROWS, tb)
            gate1 = b_ref[h].reshape(tiles, BF16_ROWS, tb)
            term = jnp.where(rank1 < row(nb_ref), row(a_ref) * gate1, jnp.zeros_like(gate1))
            gate = term if gate is None else gate + term
        gact_ref[rows, :] = gate.reshape(N_KEYS, tb) * act[rows, :]
    acc_ref[...] += jnp.dot(vt_ref[...], gact_ref[...], preferred_element_type=F32)

    @pl.when(c == pl.num_programs(1) - 1)
    def _():
        y_ref[...] = _rms(x1_ref[...] + acc_ref[...].T, gf_ref[...])


def _peer_dense(ht, u_bf, vt_bf, sel, x1, gf, tb, ec):
    t = x1.shape[0]
    a, nb, b, r1 = sel
    key_rows = ec // N_KEYS
    assert sum(DENSE_ACT_PIECES) == ec
    return pl.pallas_call(
        _peer_dense_kernel,
        grid=(t // tb, N_EXPERTS // ec),
        in_specs=[
            pl.BlockSpec((D_MODEL, tb), lambda i, c: (0, i)),
            pl.BlockSpec((ec, D_MODEL), lambda i, c: (c, 0)),
            pl.BlockSpec((D_MODEL, ec), lambda i, c: (0, c)),
            pl.BlockSpec((PEER_HEADS, key_rows, tb), lambda i, c: (0, c, i)),
            pl.BlockSpec((PEER_HEADS, key_rows, tb), lambda i, c: (0, c, i)),
            pl.BlockSpec((PEER_HEADS, N_KEYS, tb), lambda i, c: (0, 0, i)),
            pl.BlockSpec((PEER_HEADS, N_KEYS, tb), lambda i, c: (0, 0, i)),
            pl.BlockSpec((tb, D_MODEL), lambda i, c: (i, 0)),
            pl.BlockSpec((1, D_MODEL), lambda i, c: (0, 0)),
        ],
        out_specs=pl.BlockSpec((tb, D_MODEL), lambda i, c: (i, 0)),
        out_shape=jax.ShapeDtypeStruct((t, D_MODEL), F32),
        scratch_shapes=[pltpu.VMEM((D_MODEL, tb), F32), pltpu.VMEM((ec, tb), BF16), pltpu.VMEM((ec, tb), BF16)],
        compiler_params=_params("parallel", "arbitrary"),
        name="peer_dense",
    )(ht, u_bf, vt_bf, a, nb, b, r1, x1, gf)


def _token_tiles(t):
    pick = lambda pref: pref if t % pref == 0 else t
    return dict(project=pick(2048), mix=pick(256), select=pick(256), dense=pick(512))


DENSE_EXPERT_CHUNK = 2048
DENSE_ACT_PIECES = (256, 768, 1024)


def _layer_tail(z, attn, x, wts, chunk_w, chunk_b_t, seq_rows):
    tiles = _token_tiles(x.shape[0])
    x1, vln = _mix(z, attn, x, wts["lnv_g"], wts["lnv_b"], chunk_w, chunk_b_t,
                   wts["w_a"], wts["w_b"], wts["w_o"], tiles["mix"], seq_rows)
    ht, *sel = _peer_select(x1, wts["norm2_g"], wts["peer_wq"], wts["peer_subkeys"], tiles["select"])
    y = _peer_dense(ht, wts["peer_u"], wts["peer_vt"], sel, x1, wts["final_norm_g"], tiles["dense"],
                    DENSE_EXPERT_CHUNK)
    return y, vln


def kernel(x_prompt, x_sample, cache_k_w128, cache_v_w128, cache_k_w512, cache_v_w512,
           cache_k_w2048, cache_v_w2048, norm1_g, w_in, lnv_g, lnv_b, w_s, b_s, w_a, w_b, w_o,
           norm2_g, peer_wq, peer_subkeys, peer_u, peer_v, final_norm_g):
    depth = w_in.shape[0]
    assert depth == 1, "one layer: the final rmsnorm is fused into the layer's last kernel"
    batch, seq, _ = x_prompt.shape
    dec_batch, dec_seq, _ = x_sample.shape
    tp, ts = batch * seq, dec_batch * dec_seq
    assert seq % (QB * DIL_RATES[-1]) == 0 and dec_seq <= min(DIL_RATES[1:]) and CHUNK % dec_seq == 0
    assert ts == CHUNK == LANES and all(c.shape[2] == w for c, w in zip((cache_k_w128, cache_k_w512, cache_k_w2048), DIL_WINDOWS))
    l = 0
    row = lambda a: a[l].reshape(1, -1)
    wts = {
        "lnv_g": row(lnv_g), "lnv_b": row(lnv_b),
        "w_a": w_a[l].astype(BF16), "w_b": w_b[l].astype(BF16), "w_o": w_o[l].astype(BF16),
        "norm2_g": row(norm2_g), "peer_wq": peer_wq[l].astype(BF16),
        "peer_subkeys": peer_subkeys[l].astype(BF16),
        "peer_u": peer_u[l].astype(BF16), "peer_vt": peer_v[l].T.astype(BF16),
        "final_norm_g": final_norm_g.reshape(1, -1),
    }
    g1 = row(norm1_g)
    w_in_bf = w_in[l].astype(BF16)

    xp = x_prompt.reshape(tp, D_MODEL)
    cos_p, sin_p = _rope_tables(jnp.arange(seq, dtype=jnp.int32))
    zp = _project(xp, g1, w_in_bf, cos_p, sin_p, _token_tiles(tp)["project"])
    attn_p = _attn_prompt(zp, batch, seq)
    yp, vln_p = _layer_tail(zp, attn_p, xp, wts, w_s[l], jnp.transpose(b_s[l]), seq)

    xs = x_sample.reshape(ts, D_MODEL)
    pos_s = PAST_LEN + jnp.tile(jnp.arange(dec_seq, dtype=jnp.int32), dec_batch)
    cos_s, sin_s = _rope_tables(pos_s)
    zs = _project(xs, g1, w_in_bf, cos_s, sin_s, _token_tiles(ts)["project"])
    rows_last = lambda c: jnp.transpose(c[l], (0, 2, 3, 1)).reshape(dec_batch, D_B, c.shape[2])
    cache_k = [rows_last(c) for c in (cache_k_w128, cache_k_w512, cache_k_w2048)]
    cache_v = [rows_last(c) for c in (cache_v_w128, cache_v_w512, cache_v_w2048)]
    attn_s = _attn_sample(zs, dec_batch, dec_seq, cache_k, cache_v)
    zs_t = jnp.transpose(zs)
    eye = jnp.eye(CHUNK // dec_seq, dtype=F32)
    w_s_blk = jax.vmap(lambda w: jnp.kron(eye, w))(w_s[l][:, :dec_seq, :dec_seq])
    b_s_blk_t = jnp.transpose(jnp.tile(b_s[l][:, :dec_seq], (1, CHUNK // dec_seq)))
    ys, vln_s = _layer_tail(zs, attn_s, xs, wts, w_s_blk, b_s_blk_t, CHUNK)

    def heads(a, b_, rows):
        return a.reshape(1, b_, rows, B_HEADS, HEAD_DIM)

    zp3 = zp.reshape(batch, seq, D_IN)
    prompt_kv, sample_kv = [], []
    for g, w in enumerate(DIL_WINDOWS):
        keep = min(w, seq)
        for blk, cache in ((K_BLK, cache_k[g]), (V_BLK, cache_v[g])):
            cols = slice((blk + g) * COL_BLK, (blk + g + 1) * COL_BLK)
            prompt_kv.append(heads(zp3[:, seq - keep:, cols], batch, keep))
            rolled = _roll_cache(cache, zs_t, blk + g, dec_seq).reshape(dec_batch, B_HEADS, HEAD_DIM, w)
            sample_kv.append(jnp.transpose(rolled, (0, 3, 1, 2))[None])
    tail = seq - CHUNK * ((seq - 1) // CHUNK)
    prompt_gmlp_v = vln_p.reshape(1, batch, CHUNK, D_A)[:, :, CHUNK - tail:]
    sample_gmlp_v = vln_s.reshape(1, dec_batch, dec_seq, D_A)
    return (yp.reshape(batch, seq, D_MODEL), ys.reshape(dec_batch, dec_seq, D_MODEL),
            *prompt_kv, prompt_gmlp_v, *sample_kv, sample_gmlp_v)
```
